```python
import jax, jax.numpy as jnp
from jax import lax
import numpy as np

D_MODEL = 2048
BATCH = 16
SEQ = 2048
DEPTH = 2
DEC_BATCH = 16
DEC_SEQ = 64
PAST_LEN = 4096

CHUNK = 64
HEAD_DIM = 64
D_A = D_MODEL
N_HEADS_A = D_A // HEAD_DIM
R_W = 96
R_A = 96
R_G = 256
D_C = D_MODEL
CONV_K = 31
D_FF = ((8 * D_MODEL // 3 + 255) // 256) * 256
P_RWKV = 3 * D_A + R_W + R_A + R_G
P_TOT = P_RWKV + 2 * D_C + 2 * D_MODEL
EPS_RMS = 1e-6
EPS_LN = 1e-5
EPS_GN = 64e-5

kernel_name = "rwkv7_conformer_griffin_stream_step"


def rms_norm(x, g):
    xf = x.astype(jnp.float32)
    y = xf * lax.rsqrt(jnp.mean(xf * xf, axis=-1, keepdims=True) + EPS_RMS)
    return (y * g.astype(jnp.float32)).astype(x.dtype)


def swiglu_ffn(h, w1, w3, w2):
    return (jax.nn.silu(h @ w1) * (h @ w3)) @ w2


def wkv7_scan(S0, r, w, k, v, a, b):
    def step(S, inp):
        r_t, w_t, k_t, v_t, a_t, b_t = inp
        sa = jnp.einsum('bhij,bhj->bhi', S, a_t)
        S = (S * w_t[:, :, None, :] + sa[..., None] * b_t[:, :, None, :]
             + v_t[..., None] * k_t[:, :, None, :])
        y_t = jnp.einsum('bhij,bhj->bhi', S, r_t)
        return S, y_t
    xs = tuple(jnp.moveaxis(z, 1, 0) for z in (r, w, k, v, a, b))
    S, ys = lax.scan(step, S0, xs)
    return jnp.moveaxis(ys, 0, 1), S


def rwkv7_mix(p_a, shift_prev, wkv_prev, mu, w0, w_up, a0, a_up, g_up, k_k, k_a, r_k,
              gn_g, gn_b, w_o):
    f32 = jnp.float32
    B, T, _ = p_a.shape
    prev = jnp.concatenate([shift_prev[:, None].astype(p_a.dtype), p_a[:, :-1]], axis=1)
    xm = p_a + (prev - p_a) * mu
    r, k, v, xw, xa, xg = jnp.split(
        xm, [D_A, 2 * D_A, 3 * D_A, 3 * D_A + R_W, 3 * D_A + R_W + R_A], axis=-1)
    w_log = -jax.nn.softplus(-(w0 + jnp.tanh(xw) @ w_up).astype(f32)) - 0.5
    decay = jnp.exp(-jnp.exp(w_log))
    a = jax.nn.sigmoid((a0 + xa @ a_up).astype(f32))
    g = jax.nn.sigmoid(xg) @ g_up
    hs = lambda z: z.reshape(B, T, N_HEADS_A, HEAD_DIM)
    kk = hs(k.astype(f32) * k_k)
    kk = kk * lax.rsqrt(jnp.maximum(jnp.sum(kk * kk, axis=-1, keepdims=True), 1e-24))
    k = k.astype(f32) * (1.0 + (a - 1.0) * k_a)
    rh, kh, vh, ah = hs(r.astype(f32)), hs(k), hs(v.astype(f32)), hs(a)
    y, S = wkv7_scan(wkv_prev.astype(f32), rh, hs(decay), kh, vh, -kk, kk * ah)
    mean = jnp.mean(y, axis=-1, keepdims=True)
    var = jnp.mean(jnp.square(y - mean), axis=-1, keepdims=True)
    y = ((y - mean) * lax.rsqrt(var + EPS_GN)).reshape(B, T, D_A) * gn_g + gn_b
    bonus = jnp.sum(rh * kh * r_k, axis=-1, keepdims=True) * vh
    y = (y + bonus.reshape(B, T, D_A)).astype(p_a.dtype)
    return (y * g) @ w_o, S, p_a[:, -1]


def conformer_conv(p_c, conv_prev, b_in, conv_w, conv_b, ln_g, ln_b, w_o, b_o):
    val, gate = jnp.split(p_c + b_in, 2, axis=-1)
    glu = val * jax.nn.sigmoid(gate)
    xc = jnp.concatenate([conv_prev.astype(glu.dtype), glu], axis=1)
    dw = lax.conv_general_dilated(
        xc, conv_w.astype(xc.dtype)[:, None, :], window_strides=(1,), padding='VALID',
        dimension_numbers=('NWC', 'WIO', 'NWC'), feature_group_count=D_C) + conv_b
    f = dw.astype(jnp.float32)
    mean = jnp.mean(f, axis=-1, keepdims=True)
    var = jnp.mean(jnp.square(f - mean), axis=-1, keepdims=True)
    f = (f - mean) * lax.rsqrt(var + EPS_LN) * ln_g + ln_b
    h = jax.nn.silu(f).astype(p_c.dtype)
    return h @ w_o + b_o, xc[:, -(CONV_K - 1):]


def trunk_layer(x, wkv_prev, shift_prev, conv_prev, p, l):
    h = rms_norm(x, p['ffn1_norm'][l])
    x = x + 0.5 * swiglu_ffn(h, p['ffn1_w1'][l], p['ffn1_w3'][l], p['ffn1_w2'][l])
    u = rms_norm(x, p['mix_norm'][l])
    proj = u @ p['w_in'][l]
    p_a, p_c, p_g = jnp.split(proj, [P_RWKV, P_RWKV + 2 * D_C], axis=-1)
    y_a, wkv_new, shift_new = rwkv7_mix(
        p_a, shift_prev, wkv_prev, p['mu_shift'][l], p['w0'][l], p['w_up'][l], p['a0'][l],
        p['a_up'][l], p['g_up'][l], p['k_k'][l], p['k_a'][l], p['r_k'][l], p['gn_g'][l],
        p['gn_b'][l], p['w_o_a'][l])
    y_c, conv_new = conformer_conv(
        p_c, conv_prev, p['b_conv_in'][l], p['conv_w'][l], p['conv_b'][l], p['conv_ln_g'][l],
        p['conv_ln_b'][l], p['w_o_c'][l], p['b_o_c'][l])
    gate_a, gate_c = jnp.split(jax.nn.sigmoid(p_g + p['b_gate'][l]), 2, axis=-1)
    x = x + (gate_a * y_a + gate_c * y_c) @ p['w_out'][l]
    h = rms_norm(x, p['ffn2_norm'][l])
    x = x + 0.5 * swiglu_ffn(h, p['ffn2_w1'][l], p['ffn2_w3'][l], p['ffn2_w2'][l])
    return x, wkv_new, shift_new, conv_new


def run_trunk(x, wkv0, shift0, conv0, p, final_norm):
    wkvs, shifts, convs = [], [], []
    for l in range(DEPTH):
        x, s_wkv, s_shift, s_conv = trunk_layer(x, wkv0[l], shift0[l], conv0[l], p, l)
        wkvs.append(s_wkv)
        shifts.append(s_shift)
        convs.append(s_conv)
    return rms_norm(x, final_norm), jnp.stack(wkvs), jnp.stack(shifts), jnp.stack(convs)


def setup_inputs(seed: int = 0) -> dict:
    key = jax.random.key(seed)
    ks = iter(jax.random.split(key, 48))
    f32 = jnp.float32
    nrm = lambda shape, scale: scale * jax.random.normal(next(ks), shape, f32)
    gain = lambda shape: 1.0 + 0.01 * jax.random.normal(next(ks), shape, f32)
    L, D = DEPTH, D_MODEL
    return {
        "x_prompt": nrm((BATCH, SEQ, D), 1.0),
        "x_sample": nrm((DEC_BATCH, DEC_SEQ, D), 1.0),
        "state_wkv": nrm((L, DEC_BATCH, N_HEADS_A, HEAD_DIM, HEAD_DIM), 0.1),
        "state_shift": nrm((L, DEC_BATCH, P_RWKV), 1.0),
        "state_conv": nrm((L, DEC_BATCH, CONV_K - 1, D_C), 0.5),
        "ffn1_norm": gain((L, D)),
        "ffn1_w1": nrm((L, D, D_FF), D ** -0.5),
        "ffn1_w3": nrm((L, D, D_FF), D ** -0.5),
        "ffn1_w2": nrm((L, D_FF, D), D_FF ** -0.5),
        "mix_norm": gain((L, D)),
        "w_in": nrm((L, D, P_TOT), D ** -0.5),
        "mu_shift": jax.random.uniform(next(ks), (L, P_RWKV), f32),
        "w0": jax.random.uniform(next(ks), (L, D_A), f32, minval=-6.0, maxval=-1.0),
        "w_up": nrm((L, R_W, D_A), 0.5 * R_W ** -0.5),
        "a0": nrm((L, D_A), 0.5),
        "a_up": nrm((L, R_A, D_A), 0.5 * R_A ** -0.5),
        "g_up": nrm((L, R_G, D_A), R_G ** -0.5),
        "k_k": 0.85 + nrm((L, D_A), 0.05),
        "k_a": 1.0 + nrm((L, D_A), 0.05),
        "r_k": nrm((L, N_HEADS_A, HEAD_DIM), 0.1),
        "gn_g": gain((L, D_A)),
        "gn_b": nrm((L, D_A), 0.01),
        "w_o_a": nrm((L, D_A, D), D_A ** -0.5),
        "b_conv_in": nrm((L, 2 * D_C), 0.01),
        "conv_w": nrm((L, CONV_K, D_C), CONV_K ** -0.5),
        "conv_b": nrm((L, D_C), 0.01),
        "conv_ln_g": gain((L, D_C)),
        "conv_ln_b": nrm((L, D_C), 0.01),
        "w_o_c": nrm((L, D_C, D), D_C ** -0.5),
        "b_o_c": nrm((L, D), 0.01),
        "b_gate": nrm((L, 2 * D), 0.01),
        "w_out": nrm((L, D, D), D ** -0.5),
        "ffn2_norm": gain((L, D)),
        "ffn2_w1": nrm((L, D, D_FF), D ** -0.5),
        "ffn2_w3": nrm((L, D, D_FF), D ** -0.5),
        "ffn2_w2": nrm((L, D_FF, D), D_FF ** -0.5),
        "final_norm": gain((D,)),
    }


def reference(x_prompt, x_sample, state_wkv, state_shift, state_conv,
              ffn1_norm, ffn1_w1, ffn1_w3, ffn1_w2, mix_norm, w_in, mu_shift,
              w0, w_up, a0, a_up, g_up, k_k, k_a, r_k, gn_g, gn_b, w_o_a,
              b_conv_in, conv_w, conv_b, conv_ln_g, conv_ln_b, w_o_c, b_o_c,
              b_gate, w_out, ffn2_norm, ffn2_w1, ffn2_w3, ffn2_w2, final_norm):
    p = dict(ffn1_norm=ffn1_norm, ffn1_w1=ffn1_w1, ffn1_w3=ffn1_w3, ffn1_w2=ffn1_w2,
             mix_norm=mix_norm, w_in=w_in, mu_shift=mu_shift, w0=w0, w_up=w_up, a0=a0,
             a_up=a_up, g_up=g_up, k_k=k_k, k_a=k_a, r_k=r_k, gn_g=gn_g, gn_b=gn_b,
             w_o_a=w_o_a, b_conv_in=b_conv_in, conv_w=conv_w, conv_b=conv_b,
             conv_ln_g=conv_ln_g, conv_ln_b=conv_ln_b, w_o_c=w_o_c, b_o_c=b_o_c,
             b_gate=b_gate, w_out=w_out, ffn2_norm=ffn2_norm, ffn2_w1=ffn2_w1,
             ffn2_w3=ffn2_w3, ffn2_w2=ffn2_w2)
    Bp = x_prompt.shape[0]
    wkv0 = jnp.zeros((DEPTH, Bp, N_HEADS_A, HEAD_DIM, HEAD_DIM), jnp.float32)
    shift0 = jnp.zeros((DEPTH, Bp, P_RWKV), x_prompt.dtype)
    conv0 = jnp.zeros((DEPTH, Bp, CONV_K - 1, D_C), x_prompt.dtype)
    y_prompt, wkv_p, shift_p, conv_p = run_trunk(x_prompt, wkv0, shift0, conv0, p, final_norm)
    y_sample, wkv_s, shift_s, conv_s = run_trunk(x_sample, state_wkv, state_shift, state_conv,
                                                 p, final_norm)
    return (y_prompt, y_sample, wkv_p, shift_p, conv_p, wkv_s, shift_s, conv_s)
```

```python
import functools

import jax
import jax.numpy as jnp
from jax import lax
from jax.experimental import pallas as pl
from jax.experimental.pallas import tpu as pltpu

F32 = jnp.float32
BF16 = jnp.bfloat16

HEAD = 64
LANES = 128
CHUNK = 64
EPS_RMS = 1e-6
EPS_LN = 1e-5
EPS_GN = 64e-5
VMEM_LIMIT = 48 * 1024 * 1024


def _rup(a, b):
    return -(-a // b) * b


def _tile(n, pref):
    if n <= pref:
        return n
    t = pref
    while n % t:
        t //= 2
    return t


def _params(*sem):
    return pltpu.CompilerParams(dimension_semantics=sem, vmem_limit_bytes=VMEM_LIMIT)


def _dot(a, b):
    return jnp.dot(a.astype(BF16), b.astype(BF16), preferred_element_type=F32)


def _dot_nt(a, b):
    return lax.dot_general(a.astype(BF16), b.astype(BF16), (((1,), (1,)), ((), ())),
                           preferred_element_type=F32)


def _sigmoid(x):
    return 1.0 / (1.0 + jnp.exp(-x))


def _split2(x):
    hi = x.astype(BF16)
    lo = (x - hi.astype(F32)).astype(BF16)
    return hi, lo


def _dot_exact_rhs(x, e):
    hi, lo = _split2(x)
    return (jnp.dot(hi, e, preferred_element_type=F32)
            + jnp.dot(lo, e, preferred_element_type=F32))


def _rmsnorm_kernel(x_ref, g_ref, o_ref):
    x = x_ref[...]
    ms = jnp.mean(x * x, axis=-1, keepdims=True)
    o_ref[...] = (x * lax.rsqrt(ms + EPS_RMS) * g_ref[...]).astype(o_ref.dtype)


def _rmsnorm(x, g, out_dtype, name):
    m, d = x.shape
    tm = _tile(m, 512)
    return pl.pallas_call(
        _rmsnorm_kernel,
        grid=(m // tm,),
        in_specs=[pl.BlockSpec((tm, d), lambda i: (i, 0)),
                  pl.BlockSpec((1, d), lambda i: (0, 0))],
        out_specs=pl.BlockSpec((tm, d), lambda i: (i, 0)),
        out_shape=jax.ShapeDtypeStruct((m, d), out_dtype),
        compiler_params=_params("parallel"),
        name=name,
    )(x, g.reshape(1, d))


def _ffn_kernel(x_ref, g_ref, w1_ref, w3_ref, w2_ref, o_ref, h_ref, acc_ref):
    f = pl.program_id(1)

    @pl.when(f == 0)
    def _():
        x = x_ref[...]
        ms = jnp.mean(x * x, axis=-1, keepdims=True)
        h_ref[...] = (x * lax.rsqrt(ms + EPS_RMS) * g_ref[...]).astype(BF16)
        acc_ref[...] = jnp.zeros_like(acc_ref)

    h = h_ref[...]
    a = jnp.dot(h, w1_ref[...], preferred_element_type=F32)
    b = jnp.dot(h, w3_ref[...], preferred_element_type=F32)
    gated = (a * _sigmoid(a)) * b
    acc_ref[...] += jnp.dot(gated.astype(BF16), w2_ref[...], preferred_element_type=F32)

    @pl.when(f == pl.num_programs(1) - 1)
    def _():
        o_ref[...] = x_ref[...] + 0.5 * acc_ref[...]


def _ffn(x, g, w1, w3, w2, name):
    m, d = x.shape
    dff = w1.shape[1]
    tm = _tile(m, 512)
    tf = _tile(dff, 512)
    return pl.pallas_call(
        _ffn_kernel,
        grid=(m // tm, dff // tf),
        in_specs=[pl.BlockSpec((tm, d), lambda i, f: (i, 0)),
                  pl.BlockSpec((1, d), lambda i, f: (0, 0)),
                  pl.BlockSpec((d, tf), lambda i, f: (0, f)),
                  pl.BlockSpec((d, tf), lambda i, f: (0, f)),
                  pl.BlockSpec((tf, d), lambda i, f: (f, 0))],
        out_specs=pl.BlockSpec((tm, d), lambda i, f: (i, 0)),
        out_shape=jax.ShapeDtypeStruct((m, d), F32),
        scratch_shapes=[pltpu.VMEM((tm, d), BF16), pltpu.VMEM((tm, d), F32)],
        compiler_params=_params("parallel", "arbitrary"),
        name=name,
    )(x, g.reshape(1, d), w1, w3, w2)


def _mm_kernel(*refs, n_w, n_row, n_tile, epi):
    a_ref = refs[0]
    w_refs = refs[1:1 + n_w]
    row_refs = refs[1 + n_w:1 + n_w + n_row]
    tile_refs = refs[1 + n_w + n_row:1 + n_w + n_row + n_tile]
    o_ref = refs[-1]
    a = a_ref[...]
    accs = [jnp.dot(a, w[...], preferred_element_type=F32) for w in w_refs]
    out = epi(accs, [r[...] for r in row_refs], [t[...] for t in tile_refs])
    o_ref[...] = out.astype(o_ref.dtype)


def _mm(a, ws, epi, rows=(), tiles=(), out_dtype=F32, name="mm"):
    m, k = a.shape
    n = ws[0].shape[1]
    tm = _tile(m, 512)
    tn = _tile(n, 512)
    in_specs = [pl.BlockSpec((tm, k), lambda j, i: (i, 0))]
    in_specs += [pl.BlockSpec((k, tn), lambda j, i: (0, j)) for _ in ws]
    args = [a, *ws]
    for arr, off in rows:
        in_specs.append(pl.BlockSpec((1, tn), lambda j, i, o=off // tn: (0, j + o)))
        args.append(arr)
    for arr, off in tiles:
        in_specs.append(pl.BlockSpec((tm, tn), lambda j, i, o=off // tn: (i, j + o)))
        args.append(arr)
    kern = functools.partial(_mm_kernel, n_w=len(ws), n_row=len(rows), n_tile=len(tiles), epi=epi)
    return pl.pallas_call(
        kern,
        grid=(n // tn, m // tm),
        in_specs=in_specs,
        out_specs=pl.BlockSpec((tm, tn), lambda j, i: (i, j)),
        out_shape=jax.ShapeDtypeStruct((m, n), out_dtype),
        compiler_params=_params("parallel", "parallel"),
        name=name,
    )(*args)


def _proj_shift_kernel(u_ref, w_ref, mu_ref, sprev_ref, xm_ref, last_ref, carry_ref, *, seq, tm):
    i = pl.program_id(1)
    p = jnp.dot(u_ref[...], w_ref[...], preferred_element_type=F32)
    row = lax.broadcasted_iota(jnp.int32, p.shape, 0)
    prev = pltpu.roll(p, 1, 0)
    if tm >= seq:
        per_tile = tm // seq
        for s in range(per_tile):
            b = i * per_tile + s
            prev = jnp.where(row == s * seq, sprev_ref[b], prev)
            last_ref[b] = p[(s + 1) * seq - 1:(s + 1) * seq, :]
    else:
        tiles_per_seq = seq // tm
        b = i // tiles_per_seq
        t = i % tiles_per_seq
        first = jnp.where(t == 0, sprev_ref[b], carry_ref[...])
        prev = jnp.where(row == 0, first, prev)
        carry_ref[...] = p[tm - 1:tm, :]

        @pl.when(t == tiles_per_seq - 1)
        def _():
            last_ref[b] = p[tm - 1:tm, :]

    xm_ref[...] = p + (prev - p) * mu_ref[...]


def _proj_shift(u, w, mu, sprev, seq, name):
    m, k = u.shape
    n = w.shape[1]
    nb = m // seq
    tn = _tile(n, 512)
    tm = _tile(m, 512)
    if tm < seq:
        assert seq % tm == 0
    else:
        assert tm % seq == 0
    kern = functools.partial(_proj_shift_kernel, seq=seq, tm=tm)
    return pl.pallas_call(
        kern,
        grid=(n // tn, m // tm),
        in_specs=[pl.BlockSpec((tm, k), lambda j, i: (i, 0)),
                  pl.BlockSpec((k, tn), lambda j, i: (0, j)),
                  pl.BlockSpec((1, tn), lambda j, i: (0, j)),
                  pl.BlockSpec((nb, 1, tn), lambda j, i: (0, 0, j))],
        out_specs=[pl.BlockSpec((tm, tn), lambda j, i: (i, j)),
                   pl.BlockSpec((nb, 1, tn), lambda j, i: (0, 0, j))],
        out_shape=[jax.ShapeDtypeStruct((m, n), F32),
                   jax.ShapeDtypeStruct((nb, 1, n), F32)],
        scratch_shapes=[pltpu.VMEM((1, tn), F32)],
        compiler_params=_params("arbitrary", "arbitrary"),
        name=name,
    )(u, w, mu, sprev)


def _prep_kernel(xl_ref, k_ref, wup_ref, aup_ref, gup_ref, w0_ref, a0_ref, kkw_ref, kaw_ref, e_ref,
                 lw_ref, kk_ref, bb_ref, kp_ref, g_ref, *, rwp, rap):
    xl = xl_ref[...]
    xw = xl[:, :rwp]
    xa = xl[:, rwp:rwp + rap]
    xg = xl[:, rwp + rap:]
    dw = _dot(jnp.tanh(xw), wup_ref[...])
    da = _dot(xa, aup_ref[...])
    g_ref[...] = _dot(_sigmoid(xg), gup_ref[...])
    z = -(w0_ref[...] + dw)
    softplus = jnp.maximum(z, 0.0) + jnp.log(1.0 + jnp.exp(-jnp.abs(z)))
    lw_ref[...] = -jnp.exp(-softplus - 0.5)
    alpha = _sigmoid(a0_ref[...] + da)
    k = k_ref[...]
    kk = k * kkw_ref[...]
    ss = _dot_exact_rhs(kk * kk, e_ref[...])
    kk = kk * lax.rsqrt(jnp.maximum(ss, 1e-24))
    kk_ref[...] = kk
    bb_ref[...] = kk * alpha
    kp_ref[...] = k * (1.0 + (alpha - 1.0) * kaw_ref[...])


def _prep(xm_lora, xm_rkv, w_up, a_up, g_up, w0, a0, k_k, k_a, rwp, rap, name):
    m, lw_cols = xm_lora.shape
    da = w_up.shape[1]
    tm = _tile(m, 512)
    tn = _tile(da, 512)
    koff = da // tn
    idx = jnp.arange(tn) // HEAD
    e = (idx[:, None] == idx[None, :]).astype(BF16)
    row = lambda: pl.BlockSpec((1, tn), lambda i, j: (0, j))
    out = lambda: pl.BlockSpec((tm, tn), lambda i, j: (i, j))
    kern = functools.partial(_prep_kernel, rwp=rwp, rap=rap)
    return pl.pallas_call(
        kern,
        grid=(m // tm, da // tn),
        in_specs=[pl.BlockSpec((tm, lw_cols), lambda i, j: (i, 0)),
                  pl.BlockSpec((tm, tn), lambda i, j: (i, j + koff)),
                  pl.BlockSpec((rwp, tn), lambda i, j: (0, j)),
                  pl.BlockSpec((rap, tn), lambda i, j: (0, j)),
                  pl.BlockSpec((g_up.shape[0], tn), lambda i, j: (0, j)),
                  row(), row(), row(), row(),
                  pl.BlockSpec((tn, tn), lambda i, j: (0, 0))],
        out_specs=[out(), out(), out(), out(), out()],
        out_shape=[jax.ShapeDtypeStruct((m, da), F32)] * 5,
        compiler_params=_params("parallel", "parallel"),
        name=name,
    )(xm_lora, xm_rkv, w_up, a_up, g_up, w0, a0, k_k, k_a, e)


def _stack_heads(x):
    lane = lax.broadcasted_iota(jnp.int32, x.shape, 1)
    h0 = (lane & HEAD) == 0
    return jnp.concatenate([jnp.where(h0, x, 0.0), jnp.where(h0, 0.0, x)], axis=0)


def _wkv_kernel(r_ref, lw_ref, k_ref, v_ref, kk_ref, bb_ref, g_ref, s0_ref, rk_ref, gng_ref, gnb_ref,
                z_ref, sout_ref, st_ref, *, groups):
    c = pl.program_id(2)
    C, W = CHUNK, LANES

    @pl.when(c == 0)
    def _():
        st_ref[...] = s0_ref[0]

    lane = lax.broadcasted_iota(jnp.int32, (C, W), 1)
    row = lax.broadcasted_iota(jnp.int32, (C, W), 0)
    s_idx = lane & (HEAD - 1)
    strict = s_idx < row
    incl = s_idx <= row
    r2 = lax.broadcasted_iota(jnp.int32, (W, W), 0)
    l2 = lax.broadcasted_iota(jnp.int32, (W, W), 1)
    same_head = (r2 < HEAD) == (l2 < HEAD)
    ones_bd = jnp.where(same_head, 1.0, 0.0).astype(BF16)
    tri = jnp.where(lax.broadcasted_iota(jnp.int32, (C, C), 1)
                    <= lax.broadcasted_iota(jnp.int32, (C, C), 0), 1.0, 0.0).astype(BF16)

    lw_all = lw_ref[0]
    p1 = lw_all.astype(BF16)
    r1 = lw_all - p1.astype(F32)
    p2 = r1.astype(BF16)
    p3 = (r1 - p2.astype(F32)).astype(BF16)
    cum_all = (jnp.dot(tri, p1, preferred_element_type=F32)
               + jnp.dot(tri, p2, preferred_element_type=F32)
               + jnp.dot(tri, p3, preferred_element_type=F32))

    for g in range(groups):
        sl = slice(g * W, (g + 1) * W)
        r = r_ref[0, :, sl]
        k = k_ref[0, :, sl]
        v = v_ref[0, :, sl]
        kk = kk_ref[0, :, sl]
        bb = bb_ref[0, :, sl]
        lw = lw_all[:, sl]
        cum = cum_all[:, sl]
        cl = cum[C - 1:C, :]
        e_pos = jnp.exp(cum)
        e_neg = jnp.exp(-cum)
        at = -(kk * jnp.exp(cum - lw))
        rt = r * e_pos
        bt = bb * e_neg
        kt = k * e_neg
        e_hat = jnp.exp(cl - cum)
        bh = bb * e_hat
        kh = k * e_hat
        g_end = jnp.exp(cl)

        lhs = jnp.concatenate([at, rt], axis=0)
        rhs = jnp.concatenate([_stack_heads(bt), _stack_heads(kt)], axis=0)
        m1 = _dot_nt(lhs, rhs)
        ll = jnp.where(strict, m1[:C, :W], 0.0)
        aak = jnp.where(strict, m1[:C, W:], 0.0)
        arb = jnp.where(incl, m1[C:, :W], 0.0)
        ark = jnp.where(incl, m1[C:, W:], 0.0)

        sv = _stack_heads(v)
        x = jnp.concatenate([at, _dot(aak, sv)], axis=1)
        lk = ll
        for it in range(6):
            x = x + _dot(lk, _stack_heads(x))
            if it < 5:
                lk = _dot(lk, _stack_heads(lk))
        wt = x[:, :W]
        ut = x[:, W:]

        s0 = st_ref[g]
        u = _dot_nt(wt, s0) + ut
        y = _dot_nt(rt, s0) + _dot(jnp.concatenate([arb, ark], axis=1),
                                   jnp.concatenate([_stack_heads(u), sv], axis=0))
        uv = jnp.concatenate([u, v], axis=0)
        bk = jnp.concatenate([bh, kh], axis=0)
        ds = _dot(uv.T, bk)
        st_ref[g] = s0 * g_end + jnp.where(same_head, ds, 0.0)

        mean = _dot_exact_rhs(y, ones_bd) * (1.0 / HEAD)
        d = y - mean
        var = _dot_exact_rhs(d * d, ones_bd) * (1.0 / HEAD)
        yn = d * lax.rsqrt(var + EPS_GN) * gng_ref[:, sl] + gnb_ref[:, sl]
        bonus = _dot_exact_rhs(r * k * rk_ref[:, sl], ones_bd) * v
        z_ref[0, :, sl] = ((yn + bonus) * g_ref[0, :, sl]).astype(z_ref.dtype)

    @pl.when(c == pl.num_programs(2) - 1)
    def _():
        sout_ref[0] = st_ref[...]


def _wkv(xm_rkv, lw, kp, kk, bb, g, s0_bd, r_k, gn_g, gn_b, nb, seq, name):
    da = lw.shape[1]
    pairs = da // LANES
    groups = 4 if pairs % 4 == 0 else (2 if pairs % 2 == 0 else 1)
    gw = groups * LANES
    ncol = da // gw
    assert seq % CHUNK == 0
    r3 = xm_rkv.reshape(nb, seq, 3 * da)
    t3 = lambda a: a.reshape(nb, seq, da)
    tok = lambda off=0: pl.BlockSpec((1, CHUNK, gw), lambda b, j, c, o=off: (b, c, j + o))
    par = lambda: pl.BlockSpec((1, gw), lambda b, j, c: (0, j))
    st = lambda: pl.BlockSpec((1, groups, LANES, LANES), lambda b, j, c: (b, j, 0, 0))
    kern = functools.partial(_wkv_kernel, groups=groups)
    z, sout = pl.pallas_call(
        kern,
        grid=(nb, ncol, seq // CHUNK),
        in_specs=[tok(0), tok(), tok(), tok(2 * ncol), tok(), tok(), tok(), st(), par(), par(), par()],
        out_specs=[pl.BlockSpec((1, CHUNK, gw), lambda b, j, c: (b, c, j)), st()],
        out_shape=[jax.ShapeDtypeStruct((nb, seq, da), BF16),
                   jax.ShapeDtypeStruct(s0_bd.shape, F32)],
        scratch_shapes=[pltpu.VMEM((groups, LANES, LANES), F32)],
        compiler_params=_params("parallel", "parallel", "arbitrary"),
        name=name,
    )(r3, t3(lw), t3(kp), r3, t3(kk), t3(bb), t3(g), s0_bd, r_k, gn_g, gn_b)
    return z.reshape(nb * seq, da), sout


def _conv_kernel(glu_ref, prev_ref, w_ref, cb_ref, lng_ref, lnb_ref, h_ref, st_ref, xbuf, dwbuf, *, taps, tt):
    t = pl.program_id(1)
    ncb = xbuf.shape[0]
    hal = taps - 1
    top = 32 - hal
    rb = min(tt, 64)

    for cb in range(ncb):
        cs = slice(cb * LANES, (cb + 1) * LANES)

        @pl.when(t == 0)
        def _():
            xbuf[cb, top:32, :] = prev_ref[0, :, cs]

        @pl.when(t > 0)
        def _():
            xbuf[cb, top:32, :] = xbuf[cb, tt + top:tt + 32, :]

        xbuf[cb, 32:32 + tt, :] = glu_ref[0, :, cs]

    def body(cb, carry):
        for r0 in range(0, tt, rb):
            acc = jnp.zeros((rb, LANES), F32) + cb_ref[cb]
            for kx in range(taps):
                acc = acc + xbuf[cb, r0 + top + kx:r0 + top + kx + rb, :] * w_ref[cb, kx:kx + 1, :]
            dwbuf[cb, r0:r0 + rb, :] = acc
        return carry

    lax.fori_loop(0, ncb, body, 0)

    s1 = jnp.zeros((tt, LANES), F32)
    for cb in range(ncb):
        s1 = s1 + dwbuf[cb]
    dch = ncb * LANES
    mean = jnp.sum(s1, axis=-1, keepdims=True) * (1.0 / dch)
    s2 = jnp.zeros((tt, LANES), F32)
    for cb in range(ncb):
        d = dwbuf[cb] - mean
        s2 = s2 + d * d
    var = jnp.sum(s2, axis=-1, keepdims=True) * (1.0 / dch)
    inv = lax.rsqrt(var + EPS_LN)
    for cb in range(ncb):
        cs = slice(cb * LANES, (cb + 1) * LANES)
        f = (dwbuf[cb] - mean) * inv * lng_ref[:, cs] + lnb_ref[:, cs]
        h_ref[0, :, cs] = (f * _sigmoid(f)).astype(h_ref.dtype)

    @pl.when(t == pl.num_programs(1) - 1)
    def _():
        for cb in range(ncb):
            st_ref[0, :, cb * LANES:(cb + 1) * LANES] = xbuf[cb, tt + top:tt + 32, :]


def _conv(glu, conv_prev, conv_w, conv_b, ln_g, ln_b, nb, seq, name):
    dc = glu.shape[1]
    taps = conv_w.shape[0]
    assert taps - 1 <= 32
    ncb = dc // LANES
    tt = _tile(seq, 256)
    assert tt >= taps - 1
    w3 = jnp.pad(conv_w, ((0, 32 - taps), (0, 0))).reshape(32, ncb, LANES).transpose(1, 0, 2)
    cb3 = conv_b.reshape(ncb, 1, LANES)
    kern = functools.partial(_conv_kernel, taps=taps, tt=tt)
    h, st = pl.pallas_call(
        kern,
        grid=(nb, seq // tt),
        in_specs=[pl.BlockSpec((1, tt, dc), lambda b, t: (b, t, 0)),
                  pl.BlockSpec((1, taps - 1, dc), lambda b, t: (b, 0, 0)),
                  pl.BlockSpec((ncb, 32, LANES), lambda b, t: (0, 0, 0)),
                  pl.BlockSpec((ncb, 1, LANES), lambda b, t: (0, 0, 0)),
                  pl.BlockSpec((1, dc), lambda b, t: (0, 0)),
                  pl.BlockSpec((1, dc), lambda b, t: (0, 0))],
        out_specs=[pl.BlockSpec((1, tt, dc), lambda b, t: (b, t, 0)),
                   pl.BlockSpec((1, taps - 1, dc), lambda b, t: (b, 0, 0))],
        out_shape=[jax.ShapeDtypeStruct((nb, seq, dc), BF16),
                   jax.ShapeDtypeStruct((nb, taps - 1, dc), F32)],
        scratch_shapes=[pltpu.VMEM((ncb, tt + 32, LANES), F32),
                        pltpu.VMEM((ncb, tt, LANES), F32)],
        compiler_params=_params("parallel", "arbitrary"),
        name=name,
    )(glu.reshape(nb, seq, dc), conv_prev, w3, cb3, ln_g, ln_b)
    return h.reshape(nb * seq, dc), st


def _prepare_layer(p, l):
    d = p["w_in"].shape[1]
    da = p["w0"].shape[1]
    rw, ra, rg = p["w_up"].shape[1], p["a_up"].shape[1], p["g_up"].shape[1]
    rwp, rap, rgp = _rup(rw, LANES), _rup(ra, LANES), _rup(rg, LANES)
    p_rwkv = 3 * da + rw + ra + rg
    dc = p["conv_w"].shape[2]
    w_in = p["w_in"][l]
    mu = p["mu_shift"][l]
    o = 3 * da
    padc = lambda a, n: jnp.pad(a, ((0, 0), (0, n - a.shape[1])))
    padr = lambda a, n: jnp.pad(a, ((0, n - a.shape[0]), (0, 0)))
    w_lora = jnp.concatenate([padc(w_in[:, o:o + rw], rwp), padc(w_in[:, o + rw:o + rw + ra], rap),
                              padc(w_in[:, o + rw + ra:p_rwkv], rgp)], axis=1)
    mu2 = mu.reshape(1, -1)
    mu_lora = jnp.concatenate([padc(mu2[:, o:o + rw], rwp), padc(mu2[:, o + rw:o + rw + ra], rap),
                               padc(mu2[:, o + rw + ra:p_rwkv], rgp)], axis=1)
    row = lambda a: a.reshape(1, -1).astype(F32)
    q = dict(
        rw=rw, ra=ra, rg=rg, rwp=rwp, rap=rap, rgp=rgp, da=da, dc=dc,
        ffn1_norm=p["ffn1_norm"][l], ffn2_norm=p["ffn2_norm"][l], mix_norm=p["mix_norm"][l],
        ffn1_w1=p["ffn1_w1"][l].astype(BF16), ffn1_w3=p["ffn1_w3"][l].astype(BF16),
        ffn1_w2=p["ffn1_w2"][l].astype(BF16),
        ffn2_w1=p["ffn2_w1"][l].astype(BF16), ffn2_w3=p["ffn2_w3"][l].astype(BF16),
        ffn2_w2=p["ffn2_w2"][l].astype(BF16),
        w_rkv=w_in[:, :o].astype(BF16), mu_rkv=mu2[:, :o],
        w_lora=w_lora.astype(BF16), mu_lora=mu_lora,
        w_cv=w_in[:, p_rwkv:p_rwkv + dc].astype(BF16),
        w_cg=w_in[:, p_rwkv + dc:p_rwkv + 2 * dc].astype(BF16),
        b_cv=row(p["b_conv_in"][l][:dc]), b_cg=row(p["b_conv_in"][l][dc:]),
        w_gate=w_in[:, p_rwkv + 2 * dc:].astype(BF16), b_gate=row(p["b_gate"][l]),
        w_up=padr(p["w_up"][l], rwp).astype(BF16), a_up=padr(p["a_up"][l], rap).astype(BF16),
        g_up=padr(p["g_up"][l], rgp).astype(BF16),
        w0=row(p["w0"][l]), a0=row(p["a0"][l]), k_k=row(p["k_k"][l]), k_a=row(p["k_a"][l]),
        r_k=row(p["r_k"][l]), gn_g=row(p["gn_g"][l]), gn_b=row(p["gn_b"][l]),
        w_o_a=p["w_o_a"][l].astype(BF16),
        conv_w=p["conv_w"][l], conv_b=p["conv_b"][l],
        conv_ln_g=row(p["conv_ln_g"][l]), conv_ln_b=row(p["conv_ln_b"][l]),
        w_o_c=p["w_o_c"][l].astype(BF16), b_o_c=row(p["b_o_c"][l]),
        w_out=p["w_out"][l].astype(BF16),
    )
    return q


def _pad_shift(shift, q):
    da, rw, ra = q["da"], q["rw"], q["ra"]
    o = 3 * da
    padc = lambda a, n: jnp.pad(a, ((0, 0), (0, n - a.shape[1])))
    lora = jnp.concatenate([padc(shift[:, o:o + rw], q["rwp"]), padc(shift[:, o + rw:o + rw + ra], q["rap"]),
                            padc(shift[:, o + rw + ra:], q["rgp"])], axis=1)
    return shift[:, None, :o], lora[:, None, :]


def _unpad_shift(last_rkv, last_lora, q):
    rw, ra, rg, rwp, rap = q["rw"], q["ra"], q["rg"], q["rwp"], q["rap"]
    ll = last_lora[:, 0]
    return jnp.concatenate([last_rkv[:, 0], ll[:, :rw], ll[:, rwp:rwp + ra],
                            ll[:, rwp + rap:rwp + rap + rg]], axis=1)


def _layer(x, wkv_prev, shift_prev, conv_prev, q, nb, seq, tag):
    da, dc = q["da"], q["dc"]
    x = _ffn(x, q["ffn1_norm"], q["ffn1_w1"], q["ffn1_w3"], q["ffn1_w2"], name=f"ffn1_{tag}")
    u = _rmsnorm(x, q["mix_norm"], BF16, name=f"mixnorm_{tag}")

    sp_rkv, sp_lora = _pad_shift(shift_prev, q)
    xm_rkv, last_rkv = _proj_shift(u, q["w_rkv"], q["mu_rkv"], sp_rkv, seq, name=f"proj_rkv_{tag}")
    xm_lora, last_lora = _proj_shift(u, q["w_lora"], q["mu_lora"], sp_lora, seq, name=f"proj_lora_{tag}")
    shift_new = _unpad_shift(last_rkv, last_lora, q)

    glu = _mm(u, [q["w_cv"], q["w_cg"]],
              lambda accs, rows, tiles: (accs[0] + rows[0]) * _sigmoid(accs[1] + rows[1]),
              rows=[(q["b_cv"], 0), (q["b_cg"], 0)], name=f"proj_glu_{tag}")
    gates = _mm(u, [q["w_gate"]], lambda accs, rows, tiles: _sigmoid(accs[0] + rows[0]),
                rows=[(q["b_gate"], 0)], name=f"proj_gate_{tag}")

    lw, kk, bb, kp, g = _prep(xm_lora, xm_rkv, q["w_up"], q["a_up"], q["g_up"], q["w0"], q["a0"],
                              q["k_k"], q["k_a"], q["rwp"], q["rap"], name=f"prep_{tag}")
    pairs = da // LANES
    eye2 = jnp.eye(2, dtype=F32)
    s0_bd = (wkv_prev.reshape(nb, pairs, 2, HEAD, 1, HEAD) * eye2[:, None, :, None]
             ).reshape(nb, pairs, LANES, LANES)
    z, s_bd = _wkv(xm_rkv, lw, kp, kk, bb, g, s0_bd, q["r_k"], q["gn_g"], q["gn_b"], nb, seq,
                   name=f"wkv_{tag}")
    s6 = s_bd.reshape(nb, pairs, 2, HEAD, 2, HEAD)
    wkv_new = jnp.stack([s6[:, :, 0, :, 0, :], s6[:, :, 1, :, 1, :]], axis=2).reshape(wkv_prev.shape)
    y_a = _mm(z, [q["w_o_a"]], lambda accs, rows, tiles: accs[0], name=f"out_a_{tag}")

    hc, conv_new = _conv(glu, conv_prev, q["conv_w"], q["conv_b"], q["conv_ln_g"], q["conv_ln_b"],
                         nb, seq, name=f"conv_{tag}")
    mixed = _mm(hc, [q["w_o_c"]],
                lambda accs, rows, tiles: tiles[0] * tiles[2] + tiles[1] * (accs[0] + rows[0]),
                rows=[(q["b_o_c"], 0)], tiles=[(gates, 0), (gates, da), (y_a, 0)],
                out_dtype=BF16, name=f"out_c_mix_{tag}")
    x = _mm(mixed, [q["w_out"]], lambda accs, rows, tiles: tiles[0] + accs[0],
            tiles=[(x, 0)], name=f"out_{tag}")
    x = _ffn(x, q["ffn2_norm"], q["ffn2_w1"], q["ffn2_w3"], q["ffn2_w2"], name=f"ffn2_{tag}")
    return x, wkv_new, shift_new, conv_new


def _trunk(x, wkv0, shift0, conv0, layers, final_norm, tag):
    nb, seq, d = x.shape
    xf = x.reshape(nb * seq, d)
    wkvs, shifts, convs = [], [], []
    for l, q in enumerate(layers):
        xf, s_wkv, s_shift, s_conv = _layer(xf, wkv0[l], shift0[l], conv0[l], q, nb, seq, f"{tag}{l}")
        wkvs.append(s_wkv)
        shifts.append(s_shift)
        convs.append(s_conv)
    y = _rmsnorm(xf, final_norm, F32, name=f"finalnorm_{tag}").reshape(nb, seq, d)
    return y, jnp.stack(wkvs), jnp.stack(shifts), jnp.stack(convs)


def kernel(x_prompt, x_sample, state_wkv, state_shift, state_conv, ffn1_norm, ffn1_w1, ffn1_w3, ffn1_w2, mix_norm, w_in, mu_shift, w0, w_up, a0, a_up, g_up, k_k, k_a, r_k, gn_g, gn_b, w_o_a, b_conv_in, conv_w, conv_b, conv_ln_g, conv_ln_b, w_o_c, b_o_c, b_gate, w_out, ffn2_norm, ffn2_w1, ffn2_w3, ffn2_w2, final_norm):
    p = dict(ffn1_norm=ffn1_norm, ffn1_w1=ffn1_w1, ffn1_w3=ffn1_w3, ffn1_w2=ffn1_w2,
             mix_norm=mix_norm, w_in=w_in, mu_shift=mu_shift, w0=w0, w_up=w_up, a0=a0,
             a_up=a_up, g_up=g_up, k_k=k_k, k_a=k_a, r_k=r_k, gn_g=gn_g, gn_b=gn_b,
             w_o_a=w_o_a, b_conv_in=b_conv_in, conv_w=conv_w, conv_b=conv_b,
             conv_ln_g=conv_ln_g, conv_ln_b=conv_ln_b, w_o_c=w_o_c, b_o_c=b_o_c,
             b_gate=b_gate, w_out=w_out, ffn2_norm=ffn2_norm, ffn2_w1=ffn2_w1,
             ffn2_w3=ffn2_w3, ffn2_w2=ffn2_w2)
    depth = w_in.shape[0]
    layers = [_prepare_layer(p, l) for l in range(depth)]
    bp = x_prompt.shape[0]
    n_heads = state_wkv.shape[2]
    p_rwkv = state_shift.shape[2]
    taps = conv_w.shape[1]
    dc = conv_w.shape[2]
    wkv0 = jnp.zeros((depth, bp, n_heads, HEAD, HEAD), F32)
    shift0 = jnp.zeros((depth, bp, p_rwkv), F32)
    conv0 = jnp.zeros((depth, bp, taps - 1, dc), F32)
    y_p, wkv_p, shift_p, conv_p = _trunk(x_prompt, wkv0, shift0, conv0, layers, final_norm, "p")
    y_s, wkv_s, shift_s, conv_s = _trunk(x_sample, state_wkv, state_shift, state_conv, layers,
                                         final_norm, "s")
    return (y_p, y_s, wkv_p, shift_p, conv_p, wkv_s, shift_s, conv_s)
```

```python
import functools

import jax
import jax.numpy as jnp
from jax import lax
from jax.experimental import pallas as pl
from jax.experimental.pallas import tpu as pltpu

F32 = jnp.float32
BF16 = jnp.bfloat16

HEAD = 64
LANES = 128
CHUNK = 64
EPS_RMS = 1e-6
EPS_LN = 1e-5
EPS_GN = 64e-5
VMEM_LIMIT = 56 * 1024 * 1024


def _rup(a, b):
    return -(-a // b) * b


def _tile(n, pref):
    if n <= pref:
        return n
    t = pref
    while n % t:
        t //= 2
    return t


def _params(*sem):
    return pltpu.CompilerParams(dimension_semantics=sem, vmem_limit_bytes=VMEM_LIMIT)


def _dot(a, b):
    return jnp.dot(a.astype(BF16), b.astype(BF16), preferred_element_type=F32)


def _dot_nt(a, b):
    return lax.dot_general(a.astype(BF16), b.astype(BF16), (((1,), (1,)), ((), ())),
                           preferred_element_type=F32)


def _sigmoid(x):
    return 1.0 / (1.0 + jnp.exp(-x))


def _split2(x):
    hi = x.astype(BF16)
    lo = (x - hi.astype(F32)).astype(BF16)
    return hi, lo


def _dot_exact_rhs(x, e):
    hi, lo = _split2(x)
    return (jnp.dot(hi, e, preferred_element_type=F32)
            + jnp.dot(lo, e, preferred_element_type=F32))


def _ffn_kernel(x_ref, g_ref, w1_ref, w3_ref, w2_ref, g2_ref, *refs, emit_x, emit_n):
    h_ref, acc_ref = refs[-2:]
    f = pl.program_id(1)

    @pl.when(f == 0)
    def _():
        x = x_ref[...]
        ms = jnp.mean(x * x, axis=-1, keepdims=True)
        h_ref[...] = (x * lax.rsqrt(ms + EPS_RMS) * g_ref[...]).astype(BF16)
        acc_ref[...] = jnp.zeros_like(acc_ref)

    h = h_ref[...]
    a = jnp.dot(h, w1_ref[...], preferred_element_type=F32)
    b = jnp.dot(h, w3_ref[...], preferred_element_type=F32)
    gated = (a * _sigmoid(a)) * b
    acc_ref[...] += jnp.dot(gated.astype(BF16), w2_ref[...], preferred_element_type=F32)

    @pl.when(f == pl.num_programs(1) - 1)
    def _():
        o = x_ref[...] + 0.5 * acc_ref[...]
        if emit_x:
            refs[0][...] = o
        if emit_n:
            n_ref = refs[1 if emit_x else 0]
            ms = jnp.mean(o * o, axis=-1, keepdims=True)
            n_ref[...] = (o * lax.rsqrt(ms + EPS_RMS) * g2_ref[...]).astype(n_ref.dtype)


def _ffn(x, g, w1, w3, w2, g2, n_dtype, emit_x, emit_n, name):
    m, d = x.shape
    dff = w1.shape[1]
    tm = _tile(m, 512)
    tf = _tile(dff, 512)
    tok = lambda: pl.BlockSpec((tm, d), lambda i, f: (i, 0))
    row = lambda: pl.BlockSpec((1, d), lambda i, f: (0, 0))
    out_specs = [tok()] * (int(emit_x) + int(emit_n))
    out_shape = (([jax.ShapeDtypeStruct((m, d), F32)] if emit_x else [])
                 + ([jax.ShapeDtypeStruct((m, d), n_dtype)] if emit_n else []))
    outs = pl.pallas_call(
        functools.partial(_ffn_kernel, emit_x=emit_x, emit_n=emit_n),
        grid=(m // tm, dff // tf),
        in_specs=[tok(), row(),
                  pl.BlockSpec((d, tf), lambda i, f: (0, f)),
                  pl.BlockSpec((d, tf), lambda i, f: (0, f)),
                  pl.BlockSpec((tf, d), lambda i, f: (f, 0)),
                  row()],
        out_specs=out_specs,
        out_shape=out_shape,
        scratch_shapes=[pltpu.VMEM((tm, d), BF16), pltpu.VMEM((tm, d), F32)],
        compiler_params=_params("parallel", "arbitrary"),
        name=name,
    )(x, g.reshape(1, d), w1, w3, w2, g2.reshape(1, d))
    outs = list(outs)
    return (outs.pop(0) if emit_x else None), (outs.pop(0) if emit_n else None)


def _mm_kernel(*refs, n_w, n_row, n_tile, epi):
    a_ref = refs[0]
    w_refs = refs[1:1 + n_w]
    row_refs = refs[1 + n_w:1 + n_w + n_row]
    tile_refs = refs[1 + n_w + n_row:1 + n_w + n_row + n_tile]
    o_ref = refs[-1]
    a = a_ref[...]
    accs = [jnp.dot(a, w[...], preferred_element_type=F32) for w in w_refs]
    out = epi(accs, [r[...] for r in row_refs], [t[...] for t in tile_refs])
    o_ref[...] = out.astype(o_ref.dtype)


def _mm(a, ws, epi, rows=(), tiles=(), out_dtype=F32, tn_pref=1024, name="mm"):
    m, k = a.shape
    n = ws[0].shape[1]
    tm = _tile(m, 1024)
    tn = _tile(n, tn_pref)
    in_specs = [pl.BlockSpec((tm, k), lambda j, i: (i, 0))]
    in_specs += [pl.BlockSpec((k, tn), lambda j, i: (0, j)) for _ in ws]
    args = [a, *ws]
    for arr, off in rows:
        in_specs.append(pl.BlockSpec((1, tn), lambda j, i, o=off // tn: (0, j + o)))
        args.append(arr)
    for arr, off in tiles:
        in_specs.append(pl.BlockSpec((tm, tn), lambda j, i, o=off // tn: (i, j + o)))
        args.append(arr)
    kern = functools.partial(_mm_kernel, n_w=len(ws), n_row=len(rows), n_tile=len(tiles), epi=epi)
    return pl.pallas_call(
        kern,
        grid=(n // tn, m // tm),
        in_specs=in_specs,
        out_specs=pl.BlockSpec((tm, tn), lambda j, i: (i, j)),
        out_shape=jax.ShapeDtypeStruct((m, n), out_dtype),
        compiler_params=_params("parallel", "parallel"),
        name=name,
    )(*args)


def _proj_shift_kernel(u_ref, w_ref, mu_ref, sprev_ref, xm_ref, last_ref, carry_ref, *, seq, tm):
    i = pl.program_id(1)
    p = jnp.dot(u_ref[...], w_ref[...], preferred_element_type=F32)
    row = lax.broadcasted_iota(jnp.int32, p.shape, 0)
    prev = pltpu.roll(p, 1, 0)
    if tm >= seq:
        per_tile = tm // seq
        for s in range(per_tile):
            b = i * per_tile + s
            prev = jnp.where(row == s * seq, sprev_ref[b], prev)
            last_ref[b] = p[(s + 1) * seq - 1:(s + 1) * seq, :]
    else:
        tiles_per_seq = seq // tm
        b = i // tiles_per_seq
        t = i % tiles_per_seq
        first = jnp.where(t == 0, sprev_ref[b], carry_ref[...])
        prev = jnp.where(row == 0, first, prev)
        carry_ref[...] = p[tm - 1:tm, :]

        @pl.when(t == tiles_per_seq - 1)
        def _():
            last_ref[b] = p[tm - 1:tm, :]

    xm_ref[...] = p + (prev - p) * mu_ref[...]


def _proj_shift(u, w, mu, sprev, seq, name):
    m, k = u.shape
    n = w.shape[1]
    nb = m // seq
    tn = _tile(n, 1024)
    tm = _tile(m, 1024)
    if tm < seq:
        assert seq % tm == 0
    else:
        assert tm % seq == 0
    kern = functools.partial(_proj_shift_kernel, seq=seq, tm=tm)
    return pl.pallas_call(
        kern,
        grid=(n // tn, m // tm),
        in_specs=[pl.BlockSpec((tm, k), lambda j, i: (i, 0)),
                  pl.BlockSpec((k, tn), lambda j, i: (0, j)),
                  pl.BlockSpec((1, tn), lambda j, i: (0, j)),
                  pl.BlockSpec((nb, 1, tn), lambda j, i: (0, 0, j))],
        out_specs=[pl.BlockSpec((tm, tn), lambda j, i: (i, j)),
                   pl.BlockSpec((nb, 1, tn), lambda j, i: (0, 0, j))],
        out_shape=[jax.ShapeDtypeStruct((m, n), F32),
                   jax.ShapeDtypeStruct((nb, 1, n), F32)],
        scratch_shapes=[pltpu.VMEM((1, tn), F32)],
        compiler_params=_params("arbitrary", "arbitrary"),
        name=name,
    )(u, w, mu, sprev)


def _prep_kernel(xl_ref, k_ref, wup_ref, aup_ref, gup_ref, w0_ref, a0_ref, kkw_ref, kaw_ref, e_ref,
                 lw_ref, kk_ref, bb_ref, kp_ref, g_ref, *, rwp, rap):
    xl = xl_ref[...]
    xw = xl[:, :rwp]
    xa = xl[:, rwp:rwp + rap]
    xg = xl[:, rwp + rap:]
    dw = _dot(jnp.tanh(xw), wup_ref[...])
    da = _dot(xa, aup_ref[...])
    g_ref[...] = _dot(_sigmoid(xg), gup_ref[...])
    z = -(w0_ref[...] + dw)
    softplus = jnp.maximum(z, 0.0) + jnp.log(1.0 + jnp.exp(-jnp.abs(z)))
    lw_ref[...] = -jnp.exp(-softplus - 0.5)
    alpha = _sigmoid(a0_ref[...] + da)
    k = k_ref[...]
    kk = k * kkw_ref[...]
    ss = _dot_exact_rhs(kk * kk, e_ref[...])
    kk = kk * lax.rsqrt(jnp.maximum(ss, 1e-24))
    kk_ref[...] = kk
    bb_ref[...] = kk * alpha
    kp_ref[...] = k * (1.0 + (alpha - 1.0) * kaw_ref[...])


def _prep(xm_lora, xm_rkv, w_up, a_up, g_up, w0, a0, k_k, k_a, rwp, rap, name):
    m, lw_cols = xm_lora.shape
    da = w_up.shape[1]
    tm = _tile(m, 512)
    tn = _tile(da, 512)
    koff = da // tn
    idx = jnp.arange(tn) // HEAD
    e = (idx[:, None] == idx[None, :]).astype(BF16)
    row = lambda: pl.BlockSpec((1, tn), lambda i, j: (0, j))
    out = lambda: pl.BlockSpec((tm, tn), lambda i, j: (i, j))
    kern = functools.partial(_prep_kernel, rwp=rwp, rap=rap)
    return pl.pallas_call(
        kern,
        grid=(m // tm, da // tn),
        in_specs=[pl.BlockSpec((tm, lw_cols), lambda i, j: (i, 0)),
                  pl.BlockSpec((tm, tn), lambda i, j: (i, j + koff)),
                  pl.BlockSpec((rwp, tn), lambda i, j: (0, j)),
                  pl.BlockSpec((rap, tn), lambda i, j: (0, j)),
                  pl.BlockSpec((g_up.shape[0], tn), lambda i, j: (0, j)),
                  row(), row(), row(), row(),
                  pl.BlockSpec((tn, tn), lambda i, j: (0, 0))],
        out_specs=[out(), out(), out(), out(), out()],
        out_shape=[jax.ShapeDtypeStruct((m, da), F32)] * 5,
        compiler_params=_params("parallel", "parallel"),
        name=name,
    )(xm_lora, xm_rkv, w_up, a_up, g_up, w0, a0, k_k, k_a, e)


def _stack_heads(x):
    lane = lax.broadcasted_iota(jnp.int32, x.shape, 1)
    h0 = (lane & HEAD) == 0
    return jnp.concatenate([jnp.where(h0, x, 0.0), jnp.where(h0, 0.0, x)], axis=0)


def _wkv_kernel(r_ref, lw_ref, k_ref, v_ref, kk_ref, bb_ref, g_ref, s0_ref, rk_ref, gng_ref, gnb_ref,
                z_ref, sout_ref, st_ref, *, groups):
    c = pl.program_id(2)
    C, W = CHUNK, LANES

    @pl.when(c == 0)
    def _():
        st_ref[...] = s0_ref[0]

    lane = lax.broadcasted_iota(jnp.int32, (C, W), 1)
    row = lax.broadcasted_iota(jnp.int32, (C, W), 0)
    s_idx = lane & (HEAD - 1)
    strict = s_idx < row
    incl = s_idx <= row
    r2 = lax.broadcasted_iota(jnp.int32, (W, W), 0)
    l2 = lax.broadcasted_iota(jnp.int32, (W, W), 1)
    same_head = (r2 < HEAD) == (l2 < HEAD)
    ones_bd = jnp.where(same_head, 1.0, 0.0).astype(BF16)
    tri = jnp.where(lax.broadcasted_iota(jnp.int32, (C, C), 1)
                    <= lax.broadcasted_iota(jnp.int32, (C, C), 0), 1.0, 0.0).astype(BF16)

    lw_all = lw_ref[0]
    p1 = lw_all.astype(BF16)
    r1 = lw_all - p1.astype(F32)
    p2 = r1.astype(BF16)
    p3 = (r1 - p2.astype(F32)).astype(BF16)
    cum_all = (jnp.dot(tri, p1, preferred_element_type=F32)
               + jnp.dot(tri, p2, preferred_element_type=F32)
               + jnp.dot(tri, p3, preferred_element_type=F32))

    G = range(groups)
    sls = [slice(g * W, (g + 1) * W) for g in G]
    r = [r_ref[0, :, s] for s in sls]
    k = [k_ref[0, :, s] for s in sls]
    v = [v_ref[0, :, s] for s in sls]
    kk = [kk_ref[0, :, s] for s in sls]
    bb = [bb_ref[0, :, s] for s in sls]
    cum = [cum_all[:, s] for s in sls]
    cl = [cm[C - 1:C, :] for cm in cum]
    e_neg = [jnp.exp(-cm) for cm in cum]
    at = [-(kk[g] * jnp.exp(cum[g] - lw_all[:, sls[g]])) for g in G]
    rt = [r[g] * jnp.exp(cum[g]) for g in G]
    bt = [bb[g] * e_neg[g] for g in G]
    kt = [k[g] * e_neg[g] for g in G]
    e_hat = [jnp.exp(cl[g] - cum[g]) for g in G]
    bk = [jnp.concatenate([bb[g] * e_hat[g], k[g] * e_hat[g]], axis=0) for g in G]
    g_end = [jnp.exp(c_) for c_ in cl]

    m1 = [_dot_nt(jnp.concatenate([at[g], rt[g]], axis=0),
                  jnp.concatenate([_stack_heads(bt[g]), _stack_heads(kt[g])], axis=0)) for g in G]
    ll = [jnp.where(strict, m[:C, :W], 0.0) for m in m1]
    aak = [jnp.where(strict, m[:C, W:], 0.0) for m in m1]
    arbk = [jnp.concatenate([jnp.where(incl, m[C:, :W], 0.0), jnp.where(incl, m[C:, W:], 0.0)], axis=1)
            for m in m1]
    sv = [_stack_heads(x_) for x_ in v]
    x = [jnp.concatenate([at[g], _dot(aak[g], sv[g])], axis=1) for g in G]
    eye = jnp.where(s_idx == row, 1.0, 0.0)
    tinv = [eye + l_ for l_ in ll]
    lk = ll
    for _ in range(5):
        lk = [_dot(l_, _stack_heads(l_)) for l_ in lk]
        tinv = [tinv[g] + _dot(lk[g], _stack_heads(tinv[g])) for g in G]
    x = [_dot(tinv[g], _stack_heads(x[g])) for g in G]

    s0 = [st_ref[g] for g in G]
    uy = [_dot_nt(jnp.concatenate([x[g][:, :W], rt[g]], axis=0), s0[g]) for g in G]
    u = [uy[g][:C] + x[g][:, W:] for g in G]
    ds = [_dot(jnp.concatenate([u[g], v[g]], axis=0).T, bk[g]) for g in G]
    for g in G:
        st_ref[g] = s0[g] * g_end[g] + jnp.where(same_head, ds[g], 0.0)
    y = [uy[g][C:] + _dot(arbk[g], jnp.concatenate([_stack_heads(u[g]), sv[g]], axis=0))
         for g in G]

    ys = jnp.concatenate(y, axis=0)
    rkr = jnp.concatenate([r[g] * k[g] * rk_ref[:, sls[g]] for g in G], axis=0)
    sums = _dot_exact_rhs(jnp.concatenate([ys, rkr], axis=0), ones_bd)
    nrow = groups * C
    d = ys - sums[:nrow] * (1.0 / HEAD)
    var = _dot_exact_rhs(d * d, ones_bd) * (1.0 / HEAD)
    dn = d * lax.rsqrt(var + EPS_GN)
    for g in G:
        rows = slice(g * C, (g + 1) * C)
        yn = dn[rows] * gng_ref[:, sls[g]] + gnb_ref[:, sls[g]]
        bonus = sums[nrow + g * C:nrow + (g + 1) * C] * v[g]
        z_ref[0, :, sls[g]] = ((yn + bonus) * g_ref[0, :, sls[g]]).astype(z_ref.dtype)


    @pl.when(c == pl.num_programs(2) - 1)
    def _():
        sout_ref[0] = st_ref[...]


def _wkv(xm_rkv, lw, kp, kk, bb, g, s0_bd, r_k, gn_g, gn_b, nb, seq, name):
    da = lw.shape[1]
    pairs = da // LANES
    groups = next(n for n in (16, 8, 4, 2, 1) if pairs % n == 0)
    gw = groups * LANES
    ncol = da // gw
    assert seq % CHUNK == 0
    r3 = xm_rkv.reshape(nb, seq, 3 * da)
    t3 = lambda a: a.reshape(nb, seq, da)
    tok = lambda off=0: pl.BlockSpec((1, CHUNK, gw), lambda b, j, c, o=off: (b, c, j + o))
    par = lambda: pl.BlockSpec((1, gw), lambda b, j, c: (0, j))
    st = lambda: pl.BlockSpec((1, groups, LANES, LANES), lambda b, j, c: (b, j, 0, 0))
    kern = functools.partial(_wkv_kernel, groups=groups)
    z, sout = pl.pallas_call(
        kern,
        grid=(nb, ncol, seq // CHUNK),
        in_specs=[tok(0), tok(), tok(), tok(2 * ncol), tok(), tok(), tok(), st(), par(), par(), par()],
        out_specs=[pl.BlockSpec((1, CHUNK, gw), lambda b, j, c: (b, c, j)), st()],
        out_shape=[jax.ShapeDtypeStruct((nb, seq, da), BF16),
                   jax.ShapeDtypeStruct(s0_bd.shape, F32)],
        scratch_shapes=[pltpu.VMEM((groups, LANES, LANES), F32)],
        compiler_params=_params("parallel", "parallel", "arbitrary"),
        name=name,
    )(r3, t3(lw), t3(kp), r3, t3(kk), t3(bb), t3(g), s0_bd, r_k, gn_g, gn_b)
    return z.reshape(nb * seq, da), sout


def _conv_kernel(glu_ref, prev_ref, w_ref, cb_ref, lng_ref, lnb_ref, h_ref, st_ref, xbuf, dwbuf, *, taps, tt):
    t = pl.program_id(1)
    ncb = xbuf.shape[0]
    hal = taps - 1
    top = 32 - hal
    rb = min(tt, 64)

    for cb in range(ncb):
        cs = slice(cb * LANES, (cb + 1) * LANES)

        @pl.when(t == 0)
        def _():
            xbuf[cb, top:32, :] = prev_ref[0, :, cs]

        @pl.when(t > 0)
        def _():
            xbuf[cb, top:32, :] = xbuf[cb, tt + top:tt + 32, :]

        xbuf[cb, 32:32 + tt, :] = glu_ref[0, :, cs]

    def body(cb, carry):
        for r0 in range(0, tt, rb):
            acc = jnp.zeros((rb, LANES), F32) + cb_ref[cb]
            for kx in range(taps):
                acc = acc + xbuf[cb, r0 + top + kx:r0 + top + kx + rb, :] * w_ref[cb, kx:kx + 1, :]
            dwbuf[cb, r0:r0 + rb, :] = acc
        return carry

    lax.fori_loop(0, ncb, body, 0)

    s1 = jnp.zeros((tt, LANES), F32)
    for cb in range(ncb):
        s1 = s1 + dwbuf[cb]
    dch = ncb * LANES
    mean = jnp.sum(s1, axis=-1, keepdims=True) * (1.0 / dch)
    s2 = jnp.zeros((tt, LANES), F32)
    for cb in range(ncb):
        d = dwbuf[cb] - mean
        s2 = s2 + d * d
    var = jnp.sum(s2, axis=-1, keepdims=True) * (1.0 / dch)
    inv = lax.rsqrt(var + EPS_LN)
    for cb in range(ncb):
        cs = slice(cb * LANES, (cb + 1) * LANES)
        f = (dwbuf[cb] - mean) * inv * lng_ref[:, cs] + lnb_ref[:, cs]
        h_ref[0, :, cs] = (f * _sigmoid(f)).astype(h_ref.dtype)

    @pl.when(t == pl.num_programs(1) - 1)
    def _():
        for cb in range(ncb):
            st_ref[0, :, cb * LANES:(cb + 1) * LANES] = xbuf[cb, tt + top:tt + 32, :]


def _conv(glu, conv_prev, conv_w, conv_b, ln_g, ln_b, nb, seq, name):
    dc = glu.shape[1]
    taps = conv_w.shape[0]
    assert taps - 1 <= 32
    ncb = dc // LANES
    tt = _tile(seq, 256)
    assert tt >= taps - 1
    w3 = jnp.pad(conv_w, ((0, 32 - taps), (0, 0))).reshape(32, ncb, LANES).transpose(1, 0, 2)
    cb3 = conv_b.reshape(ncb, 1, LANES)
    kern = functools.partial(_conv_kernel, taps=taps, tt=tt)
    h, st = pl.pallas_call(
        kern,
        grid=(nb, seq // tt),
        in_specs=[pl.BlockSpec((1, tt, dc), lambda b, t: (b, t, 0)),
                  pl.BlockSpec((1, taps - 1, dc), lambda b, t: (b, 0, 0)),
                  pl.BlockSpec((ncb, 32, LANES), lambda b, t: (0, 0, 0)),
                  pl.BlockSpec((ncb, 1, LANES), lambda b, t: (0, 0, 0)),
                  pl.BlockSpec((1, dc), lambda b, t: (0, 0)),
                  pl.BlockSpec((1, dc), lambda b, t: (0, 0))],
        out_specs=[pl.BlockSpec((1, tt, dc), lambda b, t: (b, t, 0)),
                   pl.BlockSpec((1, taps - 1, dc), lambda b, t: (b, 0, 0))],
        out_shape=[jax.ShapeDtypeStruct((nb, seq, dc), BF16),
                   jax.ShapeDtypeStruct((nb, taps - 1, dc), F32)],
        scratch_shapes=[pltpu.VMEM((ncb, tt + 32, LANES), F32),
                        pltpu.VMEM((ncb, tt, LANES), F32)],
        compiler_params=_params("parallel", "arbitrary"),
        name=name,
    )(glu.reshape(nb, seq, dc), conv_prev, w3, cb3, ln_g, ln_b)
    return h.reshape(nb * seq, dc), st


def _prepare_layer(p, l):
    d = p["w_in"].shape[1]
    da = p["w0"].shape[1]
    rw, ra, rg = p["w_up"].shape[1], p["a_up"].shape[1], p["g_up"].shape[1]
    rwp, rap, rgp = _rup(rw, LANES), _rup(ra, LANES), _rup(rg, LANES)
    p_rwkv = 3 * da + rw + ra + rg
    dc = p["conv_w"].shape[2]
    w_in = p["w_in"][l]
    mu = p["mu_shift"][l]
    o = 3 * da
    padc = lambda a, n: jnp.pad(a, ((0, 0), (0, n - a.shape[1])))
    padr = lambda a, n: jnp.pad(a, ((0, n - a.shape[0]), (0, 0)))
    w_lora = jnp.concatenate([padc(w_in[:, o:o + rw], rwp), padc(w_in[:, o + rw:o + rw + ra], rap),
                              padc(w_in[:, o + rw + ra:p_rwkv], rgp)], axis=1)
    mu2 = mu.reshape(1, -1)
    mu_lora = jnp.concatenate([padc(mu2[:, o:o + rw], rwp), padc(mu2[:, o + rw:o + rw + ra], rap),
                               padc(mu2[:, o + rw + ra:p_rwkv], rgp)], axis=1)
    row = lambda a: a.reshape(1, -1).astype(F32)
    q = dict(
        rw=rw, ra=ra, rg=rg, rwp=rwp, rap=rap, rgp=rgp, da=da, dc=dc,
        ffn1_norm=p["ffn1_norm"][l], ffn2_norm=p["ffn2_norm"][l], mix_norm=p["mix_norm"][l],
        ffn1_w1=p["ffn1_w1"][l].astype(BF16), ffn1_w3=p["ffn1_w3"][l].astype(BF16),
        ffn1_w2=p["ffn1_w2"][l].astype(BF16),
        ffn2_w1=p["ffn2_w1"][l].astype(BF16), ffn2_w3=p["ffn2_w3"][l].astype(BF16),
        ffn2_w2=p["ffn2_w2"][l].astype(BF16),
        w_rkv=w_in[:, :o].astype(BF16), mu_rkv=mu2[:, :o],
        w_lora=w_lora.astype(BF16), mu_lora=mu_lora,
        w_cv=w_in[:, p_rwkv:p_rwkv + dc].astype(BF16),
        w_cg=w_in[:, p_rwkv + dc:p_rwkv + 2 * dc].astype(BF16),
        b_cv=row(p["b_conv_in"][l][:dc]), b_cg=row(p["b_conv_in"][l][dc:]),
        w_gate=w_in[:, p_rwkv + 2 * dc:].astype(BF16), b_gate=row(p["b_gate"][l]),
        w_up=padr(p["w_up"][l], rwp).astype(BF16), a_up=padr(p["a_up"][l], rap).astype(BF16),
        g_up=padr(p["g_up"][l], rgp).astype(BF16),
        w0=row(p["w0"][l]), a0=row(p["a0"][l]), k_k=row(p["k_k"][l]), k_a=row(p["k_a"][l]),
        r_k=row(p["r_k"][l]), gn_g=row(p["gn_g"][l]), gn_b=row(p["gn_b"][l]),
        w_o_a=p["w_o_a"][l].astype(BF16),
        conv_w=p["conv_w"][l], conv_b=p["conv_b"][l],
        conv_ln_g=row(p["conv_ln_g"][l]), conv_ln_b=row(p["conv_ln_b"][l]),
        w_o_c=p["w_o_c"][l].astype(BF16), b_o_c=row(p["b_o_c"][l]),
        w_out=p["w_out"][l].astype(BF16),
    )
    return q


def _pad_shift(shift, q):
    da, rw, ra = q["da"], q["rw"], q["ra"]
    o = 3 * da
    padc = lambda a, n: jnp.pad(a, ((0, 0), (0, n - a.shape[1])))
    lora = jnp.concatenate([padc(shift[:, o:o + rw], q["rwp"]), padc(shift[:, o + rw:o + rw + ra], q["rap"]),
                            padc(shift[:, o + rw + ra:], q["rgp"])], axis=1)
    return shift[:, None, :o], lora[:, None, :]


def _unpad_shift(last_rkv, last_lora, q):
    rw, ra, rg, rwp, rap = q["rw"], q["ra"], q["rg"], q["rwp"], q["rap"]
    ll = last_lora[:, 0]
    return jnp.concatenate([last_rkv[:, 0], ll[:, :rw], ll[:, rwp:rwp + ra],
                            ll[:, rwp + rap:rwp + rap + rg]], axis=1)


def _layer(x, wkv_prev, shift_prev, conv_prev, q, nb, seq, final_norm, tag):
    da, dc = q["da"], q["dc"]
    x, u = _ffn(x, q["ffn1_norm"], q["ffn1_w1"], q["ffn1_w3"], q["ffn1_w2"], q["mix_norm"], BF16,
                True, True, name=f"ffn1_{tag}")

    sp_rkv, sp_lora = _pad_shift(shift_prev, q)
    xm_rkv, last_rkv = _proj_shift(u, q["w_rkv"], q["mu_rkv"], sp_rkv, seq, name=f"proj_rkv_{tag}")
    xm_lora, last_lora = _proj_shift(u, q["w_lora"], q["mu_lora"], sp_lora, seq, name=f"proj_lora_{tag}")
    shift_new = _unpad_shift(last_rkv, last_lora, q)

    glu = _mm(u, [q["w_cv"], q["w_cg"]],
              lambda accs, rows, tiles: (accs[0] + rows[0]) * _sigmoid(accs[1] + rows[1]),
              rows=[(q["b_cv"], 0), (q["b_cg"], 0)], tn_pref=512, name=f"proj_glu_{tag}")
    gates = _mm(u, [q["w_gate"]], lambda accs, rows, tiles: _sigmoid(accs[0] + rows[0]),
                rows=[(q["b_gate"], 0)], out_dtype=BF16, name=f"proj_gate_{tag}")

    lw, kk, bb, kp, g = _prep(xm_lora, xm_rkv, q["w_up"], q["a_up"], q["g_up"], q["w0"], q["a0"],
                              q["k_k"], q["k_a"], q["rwp"], q["rap"], name=f"prep_{tag}")
    pairs = da // LANES
    eye2 = jnp.eye(2, dtype=F32)
    s0_bd = (wkv_prev.reshape(nb, pairs, 2, HEAD, 1, HEAD) * eye2[:, None, :, None]
             ).reshape(nb, pairs, LANES, LANES)
    z, s_bd = _wkv(xm_rkv, lw, kp, kk, bb, g, s0_bd, q["r_k"], q["gn_g"], q["gn_b"], nb, seq,
                   name=f"wkv_{tag}")
    s6 = s_bd.reshape(nb, pairs, 2, HEAD, 2, HEAD)
    wkv_new = jnp.stack([s6[:, :, 0, :, 0, :], s6[:, :, 1, :, 1, :]], axis=2).reshape(wkv_prev.shape)
    y_a = _mm(z, [q["w_o_a"]], lambda accs, rows, tiles: accs[0], name=f"out_a_{tag}")

    hc, conv_new = _conv(glu, conv_prev, q["conv_w"], q["conv_b"], q["conv_ln_g"], q["conv_ln_b"],
                         nb, seq, name=f"conv_{tag}")
    mixed = _mm(hc, [q["w_o_c"]],
                lambda accs, rows, tiles: tiles[0] * tiles[2] + tiles[1] * (accs[0] + rows[0]),
                rows=[(q["b_o_c"], 0)], tiles=[(gates, 0), (gates, da), (y_a, 0)],
                out_dtype=BF16, name=f"out_c_mix_{tag}")
    x = _mm(mixed, [q["w_out"]], lambda accs, rows, tiles: tiles[0] + accs[0],
            tiles=[(x, 0)], name=f"out_{tag}")
    last = final_norm is not None
    x, y = _ffn(x, q["ffn2_norm"], q["ffn2_w1"], q["ffn2_w3"], q["ffn2_w2"],
                final_norm if last else q["ffn2_norm"], F32, not last, last, name=f"ffn2_{tag}")
    return (y if last else x), wkv_new, shift_new, conv_new


def _trunk(x, wkv0, shift0, conv0, layers, final_norm, tag):
    nb, seq, d = x.shape
    xf = x.reshape(nb * seq, d)
    wkvs, shifts, convs = [], [], []
    for l, q in enumerate(layers):
        fin = final_norm if l == len(layers) - 1 else None
        xf, s_wkv, s_shift, s_conv = _layer(xf, wkv0[l], shift0[l], conv0[l], q, nb, seq, fin, f"{tag}{l}")
        wkvs.append(s_wkv)
        shifts.append(s_shift)
        convs.append(s_conv)
    return xf.reshape(nb, seq, d), jnp.stack(wkvs), jnp.stack(shifts), jnp.stack(convs)


def kernel(x_prompt, x_sample, state_wkv, state_shift, state_conv, ffn1_norm, ffn1_w1, ffn1_w3, ffn1_w2, mix_norm, w_in, mu_shift, w0, w_up, a0, a_up, g_up, k_k, k_a, r_k, gn_g, gn_b, w_o_a, b_conv_in, conv_w, conv_b, conv_ln_g, conv_ln_b, w_o_c, b_o_c, b_gate, w_out, ffn2_norm, ffn2_w1, ffn2_w3, ffn2_w2, final_norm):
    p = dict(ffn1_norm=ffn1_norm, ffn1_w1=ffn1_w1, ffn1_w3=ffn1_w3, ffn1_w2=ffn1_w2,
             mix_norm=mix_norm, w_in=w_in, mu_shift=mu_shift, w0=w0, w_up=w_up, a0=a0,
             a_up=a_up, g_up=g_up, k_k=k_k, k_a=k_a, r_k=r_k, gn_g=gn_g, gn_b=gn_b,
             w_o_a=w_o_a, b_conv_in=b_conv_in, conv_w=conv_w, conv_b=conv_b,
             conv_ln_g=conv_ln_g, conv_ln_b=conv_ln_b, w_o_c=w_o_c, b_o_c=b_o_c,
             b_gate=b_gate, w_out=w_out, ffn2_norm=ffn2_norm, ffn2_w1=ffn2_w1,
             ffn2_w3=ffn2_w3, ffn2_w2=ffn2_w2)
    depth = w_in.shape[0]
    layers = [_prepare_layer(p, l) for l in range(depth)]
    bp = x_prompt.shape[0]
    n_heads = state_wkv.shape[2]
    p_rwkv = state_shift.shape[2]
    taps = conv_w.shape[1]
    dc = conv_w.shape[2]
    wkv0 = jnp.zeros((depth, bp, n_heads, HEAD, HEAD), F32)
    shift0 = jnp.zeros((depth, bp, p_rwkv), F32)
    conv0 = jnp.zeros((depth, bp, taps - 1, dc), F32)
    y_p, wkv_p, shift_p, conv_p = _trunk(x_prompt, wkv0, shift0, conv0, layers, final_norm, "p")
    y_s, wkv_s, shift_s, conv_s = _trunk(x_sample, state_wkv, state_shift, state_conv, layers,
                                         final_norm, "s")
    return (y_p, y_s, wkv_p, shift_p, conv_p, wkv_s, shift_s, conv_s)
```

```python
import functools

import jax
import jax.numpy as jnp
from jax import lax
from jax.experimental import pallas as pl
from jax.experimental.pallas import tpu as pltpu

F32 = jnp.float32
BF16 = jnp.bfloat16

HEAD = 64
LANES = 128
CHUNK = 64
EPS_RMS = 1e-6
EPS_LN = 1e-5
EPS_GN = 64e-5
VMEM_LIMIT = 56 * 1024 * 1024


def _rup(a, b):
    return -(-a // b) * b


def _tile(n, pref):
    if n <= pref:
        return n
    t = pref
    while n % t:
        t //= 2
    return t


def _params(*sem):
    return pltpu.CompilerParams(dimension_semantics=sem, vmem_limit_bytes=VMEM_LIMIT)


def _dot(a, b):
    return jnp.dot(a.astype(BF16), b.astype(BF16), preferred_element_type=F32)


def _dot_nt(a, b):
    return lax.dot_general(a.astype(BF16), b.astype(BF16), (((1,), (1,)), ((), ())),
                           preferred_element_type=F32)


def _sigmoid(x):
    return 1.0 / (1.0 + jnp.exp(-x))


def _ffn_kernel(x_ref, g_ref, w1_ref, w3_ref, w2_ref, g2_ref, *refs, emit_x, emit_n):
    h_ref, acc_ref = refs[-2:]
    f = pl.program_id(1)

    @pl.when(f == 0)
    def _():
        x = x_ref[...]
        ms = jnp.mean(x * x, axis=-1, keepdims=True)
        h_ref[...] = (x * lax.rsqrt(ms + EPS_RMS) * g_ref[...]).astype(BF16)
        acc_ref[...] = jnp.zeros_like(acc_ref)

    h = h_ref[...]
    a = jnp.dot(h, w1_ref[...], preferred_element_type=F32)
    b = jnp.dot(h, w3_ref[...], preferred_element_type=F32)
    gated = (a * _sigmoid(a)) * b
    acc_ref[...] += jnp.dot(gated.astype(BF16), w2_ref[...], preferred_element_type=F32)

    @pl.when(f == pl.num_programs(1) - 1)
    def _():
        o = x_ref[...] + 0.5 * acc_ref[...]
        if emit_x:
            refs[0][...] = o
        if emit_n:
            n_ref = refs[1 if emit_x else 0]
            ms = jnp.mean(o * o, axis=-1, keepdims=True)
            n_ref[...] = (o * lax.rsqrt(ms + EPS_RMS) * g2_ref[...]).astype(n_ref.dtype)


def _ffn(x, g, w1, w3, w2, g2, n_dtype, emit_x, emit_n, name):
    m, d = x.shape
    dff = w1.shape[1]
    tm = _tile(m, 512)
    tf = _tile(dff, 512)
    tok = lambda: pl.BlockSpec((tm, d), lambda i, f: (i, 0))
    row = lambda: pl.BlockSpec((1, d), lambda i, f: (0, 0))
    out_specs = [tok()] * (int(emit_x) + int(emit_n))
    out_shape = (([jax.ShapeDtypeStruct((m, d), F32)] if emit_x else [])
                 + ([jax.ShapeDtypeStruct((m, d), n_dtype)] if emit_n else []))
    outs = pl.pallas_call(
        functools.partial(_ffn_kernel, emit_x=emit_x, emit_n=emit_n),
        grid=(m // tm, dff // tf),
        in_specs=[tok(), row(),
                  pl.BlockSpec((d, tf), lambda i, f: (0, f)),
                  pl.BlockSpec((d, tf), lambda i, f: (0, f)),
                  pl.BlockSpec((tf, d), lambda i, f: (f, 0)),
                  row()],
        out_specs=out_specs,
        out_shape=out_shape,
        scratch_shapes=[pltpu.VMEM((tm, d), BF16), pltpu.VMEM((tm, d), F32)],
        compiler_params=_params("parallel", "arbitrary"),
        name=name,
    )(x, g.reshape(1, d), w1, w3, w2, g2.reshape(1, d))
    outs = list(outs)
    return (outs.pop(0) if emit_x else None), (outs.pop(0) if emit_n else None)


def _mm_kernel(*refs, n_w, n_row, n_tile, epi):
    a_ref = refs[0]
    w_refs = refs[1:1 + n_w]
    row_refs = refs[1 + n_w:1 + n_w + n_row]
    tile_refs = refs[1 + n_w + n_row:1 + n_w + n_row + n_tile]
    o_ref = refs[-1]
    a = a_ref[...]
    accs = [jnp.dot(a, w[...], preferred_element_type=F32) for w in w_refs]
    out = epi(accs, [r[...] for r in row_refs], [t[...] for t in tile_refs])
    o_ref[...] = out.astype(o_ref.dtype)


def _mm(a, ws, epi, rows=(), tiles=(), out_dtype=F32, tn_pref=1024, name="mm"):
    m, k = a.shape
    n = ws[0].shape[1]
    tm = _tile(m, 1024)
    tn = _tile(n, tn_pref)
    in_specs = [pl.BlockSpec((tm, k), lambda j, i: (i, 0))]
    in_specs += [pl.BlockSpec((k, tn), lambda j, i: (0, j)) for _ in ws]
    args = [a, *ws]
    for arr, off in rows:
        in_specs.append(pl.BlockSpec((1, tn), lambda j, i, o=off // tn: (0, j + o)))
        args.append(arr)
    for arr, off in tiles:
        in_specs.append(pl.BlockSpec((tm, tn), lambda j, i, o=off // tn: (i, j + o)))
        args.append(arr)
    kern = functools.partial(_mm_kernel, n_w=len(ws), n_row=len(rows), n_tile=len(tiles), epi=epi)
    return pl.pallas_call(
        kern,
        grid=(n // tn, m // tm),
        in_specs=in_specs,
        out_specs=pl.BlockSpec((tm, tn), lambda j, i: (i, j)),
        out_shape=jax.ShapeDtypeStruct((m, n), out_dtype),
        compiler_params=_params("parallel", "parallel"),
        name=name,
    )(*args)


def _proj_shift_kernel(u_ref, w_ref, mu_ref, sprev_ref, xm_ref, last_ref, carry_ref, *, seq, tm):
    i = pl.program_id(1)
    p = jnp.dot(u_ref[...], w_ref[...], preferred_element_type=F32)
    row = lax.broadcasted_iota(jnp.int32, p.shape, 0)
    prev = pltpu.roll(p, 1, 0)
    if tm >= seq:
        per_tile = tm // seq
        for s in range(per_tile):
            b = i * per_tile + s
            prev = jnp.where(row == s * seq, sprev_ref[b], prev)
            last_ref[b] = p[(s + 1) * seq - 1:(s + 1) * seq, :]
    else:
        tiles_per_seq = seq // tm
        b = i // tiles_per_seq
        t = i % tiles_per_seq
        first = jnp.where(t == 0, sprev_ref[b], carry_ref[...])
        prev = jnp.where(row == 0, first, prev)
        carry_ref[...] = p[tm - 1:tm, :]

        @pl.when(t == tiles_per_seq - 1)
        def _():
            last_ref[b] = p[tm - 1:tm, :]

    xm_ref[...] = (p + (prev - p) * mu_ref[...]).astype(xm_ref.dtype)


def _proj_shift(u, w, mu, sprev, seq, out_dtype, name):
    m, k = u.shape
    n = w.shape[1]
    nb = m // seq
    tn = _tile(n, 1024)
    tm = _tile(m, 1024)
    if tm < seq:
        assert seq % tm == 0
    else:
        assert tm % seq == 0
    kern = functools.partial(_proj_shift_kernel, seq=seq, tm=tm)
    return pl.pallas_call(
        kern,
        grid=(n // tn, m // tm),
        in_specs=[pl.BlockSpec((tm, k), lambda j, i: (i, 0)),
                  pl.BlockSpec((k, tn), lambda j, i: (0, j)),
                  pl.BlockSpec((1, tn), lambda j, i: (0, j)),
                  pl.BlockSpec((nb, 1, tn), lambda j, i: (0, 0, j))],
        out_specs=[pl.BlockSpec((tm, tn), lambda j, i: (i, j)),
                   pl.BlockSpec((nb, 1, tn), lambda j, i: (0, 0, j))],
        out_shape=[jax.ShapeDtypeStruct((m, n), out_dtype),
                   jax.ShapeDtypeStruct((nb, 1, n), F32)],
        scratch_shapes=[pltpu.VMEM((1, tn), F32)],
        compiler_params=_params("arbitrary", "arbitrary"),
        name=name,
    )(u, w, mu, sprev)


def _pair_sums(x):
    h0 = lax.broadcasted_iota(jnp.int32, x.shape, 1) < HEAD
    s0 = jnp.sum(jnp.where(h0, x, 0.0), axis=-1, keepdims=True)
    s1 = jnp.sum(jnp.where(h0, 0.0, x), axis=-1, keepdims=True)
    return jnp.where(h0, s0, s1)


def _stack_heads(x):
    lane = lax.broadcasted_iota(jnp.int32, x.shape, 1)
    h0 = (lane & HEAD) == 0
    return jnp.concatenate([jnp.where(h0, x, 0.0), jnp.where(h0, 0.0, x)], axis=0)


def _wkv_kernel(*refs, groups, sub, rwp, rap, has_state):
    (xr_ref, xk_ref, xv_ref, xl_ref, wup_ref, aup_ref, gup_ref,
     w0_ref, a0_ref, kkw_ref, kaw_ref, rk_ref, gng_ref, gnb_ref) = refs[:14]
    s0_ref = refs[14] if has_state else None
    z_ref, sout_ref, st_ref = refs[-3:]
    c = pl.program_id(2)
    C, W = CHUNK, LANES
    G = range(groups)
    sls = [slice(g * W, (g + 1) * W) for g in G]

    @pl.when(c == 0)
    def _():
        if has_state:
            zero = jnp.zeros((HEAD, HEAD), F32)
            for g in G:
                top = jnp.concatenate([s0_ref[0, 2 * g], zero], axis=1)
                bot = jnp.concatenate([zero, s0_ref[0, 2 * g + 1]], axis=1)
                st_ref[g] = jnp.concatenate([top, bot], axis=0)
        else:
            st_ref[...] = jnp.zeros_like(st_ref)

    lane = lax.broadcasted_iota(jnp.int32, (C, W), 1)
    row = lax.broadcasted_iota(jnp.int32, (C, W), 0)
    s_idx = lane & (HEAD - 1)
    strict = s_idx < row
    incl = s_idx <= row
    eye = jnp.where(s_idx == row, 1.0, 0.0)
    same_head = ((lax.broadcasted_iota(jnp.int32, (W, W), 0) < HEAD)
                 == (lax.broadcasted_iota(jnp.int32, (W, W), 1) < HEAD))
    tri = jnp.where(lax.broadcasted_iota(jnp.int32, (C, C), 1)
                    <= lax.broadcasted_iota(jnp.int32, (C, C), 0), 1.0, 0.0).astype(BF16)

    def cumsum(x):
        p1 = x.astype(BF16)
        r1 = x - p1.astype(F32)
        p2 = r1.astype(BF16)
        p3 = (r1 - p2.astype(F32)).astype(BF16)
        return (jnp.dot(tri, p1, preferred_element_type=F32)
                + jnp.dot(tri, p2, preferred_element_type=F32)
                + jnp.dot(tri, p3, preferred_element_type=F32))

    class Chunk:
        N_STAGES = 10

        def __init__(self, ci):
            self.rows = slice(ci * C, (ci + 1) * C)
            self.p = [dict() for _ in G]

        def head_common(self):
            xl = xl_ref[0, self.rows, :]
            dw = _dot(jnp.tanh(xl[:, :rwp]), wup_ref[...])
            da = _dot(xl[:, rwp:rwp + rap], aup_ref[...])
            self.gate = _dot(_sigmoid(xl[:, rwp + rap:]), gup_ref[...])
            zz = -(w0_ref[...] + dw)
            softplus = jnp.maximum(zz, 0.0) + jnp.log(1.0 + jnp.exp(-jnp.abs(zz)))
            self.lw = -jnp.exp(-softplus - 0.5)
            self.alpha = _sigmoid(a0_ref[...] + da)
            self.cum = cumsum(self.lw)

        def head(self, g):
            sl, p = sls[g], self.p[g]
            r = xr_ref[0, self.rows, sl].astype(F32)
            v = xv_ref[0, self.rows, sl].astype(F32)
            k_raw = xk_ref[0, self.rows, sl].astype(F32)
            alpha = self.alpha[:, sl]
            k = k_raw * (1.0 + (alpha - 1.0) * kaw_ref[:, sl])
            kk = k_raw * kkw_ref[:, sl]
            kk = kk * lax.rsqrt(jnp.maximum(_pair_sums(kk * kk), 1e-24))
            bb = kk * alpha
            cum = self.cum[:, sl]
            cl = cum[C - 1:C, :]
            e_neg = jnp.exp(-cum)
            e_hat = jnp.exp(cl - cum)
            at = -(kk * jnp.exp(cum - self.lw[:, sl]))
            rt = r * jnp.exp(cum)
            p.update(
                v=v, at=at, rt=rt.astype(BF16), g_end=jnp.exp(cl), rkr=r * k * rk_ref[:, sl],
                lhs=jnp.concatenate([at, rt], axis=0).astype(BF16),
                rhs=jnp.concatenate([_stack_heads(bb * e_neg), _stack_heads(k * e_neg)], axis=0).astype(BF16),
                bk=jnp.concatenate([bb * e_hat, k * e_hat], axis=0).astype(BF16),
                sv=_stack_heads(v).astype(BF16))

        def stage(self, n, s):
            for g in G:
                p = self.p[g]
                if n == 0:
                    m1 = _dot_nt(p["lhs"], p["rhs"])
                    p["ll"] = jnp.where(strict, m1[:C, :W], 0.0)
                    p["aak"] = jnp.where(strict, m1[:C, W:], 0.0).astype(BF16)
                    p["arbk"] = jnp.concatenate([jnp.where(incl, m1[C:, :W], 0.0),
                                                 jnp.where(incl, m1[C:, W:], 0.0)], axis=1).astype(BF16)
                elif n == 1:
                    p["x"] = jnp.concatenate([p["at"], _dot(p["aak"], p["sv"])], axis=1)
                    p["tinv"] = eye + p["ll"]
                    p["lk"] = _dot(p["ll"], _stack_heads(p["ll"]))
                elif n <= 6:
                    p["tinv"] = p["tinv"] + _dot(p["lk"], _stack_heads(p["tinv"]))
                    if n < 6:
                        p["lk"] = _dot(p["lk"], _stack_heads(p["lk"]))
                elif n == 7:
                    p["x"] = _dot(p["tinv"], _stack_heads(p["x"]))
                elif n == 8:
                    uy = _dot_nt(jnp.concatenate([p["x"][:, :W].astype(BF16), p["rt"]], axis=0), s[g])
                    p["u"] = uy[:C] + p["x"][:, W:]
                    p["y"] = uy[C:]
                else:
                    u = p["u"]
                    ds = _dot(jnp.concatenate([u, p["v"]], axis=0).T, p["bk"])
                    s[g] = s[g] * p["g_end"] + jnp.where(same_head, ds, 0.0)
                    p["y"] = p["y"] + _dot(p["arbk"],
                                           jnp.concatenate([_stack_heads(u).astype(BF16), p["sv"]], axis=0))

        def post(self, g):
            sl, p = sls[g], self.p[g]
            y = p["y"]
            d = y - _pair_sums(y) * (1.0 / HEAD)
            var = _pair_sums(d * d) * (1.0 / HEAD)
            yn = d * lax.rsqrt(var + EPS_GN) * gng_ref[:, sl] + gnb_ref[:, sl]
            bonus = _pair_sums(p["rkr"]) * p["v"]
            z_ref[0, self.rows, sl] = ((yn + bonus) * self.gate[:, sl]).astype(z_ref.dtype)

    chunks = [Chunk(ci) for ci in range(sub)]
    s = [st_ref[g] for g in G]
    chunks[0].head_common()
    for g in G:
        chunks[0].head(g)
    for ci, ch in enumerate(chunks):
        side = []
        if ci > 0:
            side += [functools.partial(chunks[ci - 1].post, g) for g in G]
        if ci + 1 < sub:
            side += [chunks[ci + 1].head_common] + [functools.partial(chunks[ci + 1].head, g) for g in G]
        for n in range(Chunk.N_STAGES):
            ch.stage(n, s)
            for piece in side[len(side) * n // Chunk.N_STAGES:len(side) * (n + 1) // Chunk.N_STAGES]:
                piece()
    for g in G:
        chunks[-1].post(g)
        st_ref[g] = s[g]

    @pl.when(c == pl.num_programs(2) - 1)
    def _():
        for g in G:
            sg = st_ref[g]
            sout_ref[0, 2 * g] = sg[:HEAD, :HEAD]
            sout_ref[0, 2 * g + 1] = pltpu.roll(sg[HEAD:, :], HEAD, 1)[:, :HEAD]


def _wkv(xm_rkv, xm_lora, q, wkv_prev, nb, seq, name):
    da = q["da"]
    lora = xm_lora.shape[1]
    pairs = da // LANES
    groups = next(n for n in (16, 8, 4, 2, 1) if pairs % n == 0)
    sub = next(n for n in (4, 2, 1) if seq % (n * CHUNK) == 0)
    gw = groups * LANES
    ncol = da // gw
    tt = sub * CHUNK
    assert seq % tt == 0
    has_state = wkv_prev is not None
    r3 = xm_rkv.reshape(nb, seq, 3 * da)
    tok = lambda off: pl.BlockSpec((1, tt, gw), lambda b, j, c, o=off: (b, c, j + o))
    col = lambda rows: pl.BlockSpec((rows, gw), lambda b, j, c: (0, j))
    st = lambda: pl.BlockSpec((1, 2 * groups, HEAD, HEAD), lambda b, j, c: (b, j, 0, 0))
    in_specs = [tok(0), tok(ncol), tok(2 * ncol),
                pl.BlockSpec((1, tt, lora), lambda b, j, c: (b, c, 0)),
                col(q["rwp"]), col(q["rap"]), col(q["rgp"])] + [col(1)] * 7
    args = [r3, r3, r3, xm_lora.reshape(nb, seq, lora), q["w_up"], q["a_up"], q["g_up"],
            q["w0"], q["a0"], q["k_k"], q["k_a"], q["r_k"], q["gn_g"], q["gn_b"]]
    if has_state:
        in_specs.append(st())
        args.append(wkv_prev)
    kern = functools.partial(_wkv_kernel, groups=groups, sub=sub, rwp=q["rwp"], rap=q["rap"],
                             has_state=has_state)
    z, sout = pl.pallas_call(
        kern,
        grid=(nb, ncol, seq // tt),
        in_specs=in_specs,
        out_specs=[pl.BlockSpec((1, tt, gw), lambda b, j, c: (b, c, j)), st()],
        out_shape=[jax.ShapeDtypeStruct((nb, seq, da), BF16),
                   jax.ShapeDtypeStruct((nb, 2 * pairs, HEAD, HEAD), F32)],
        scratch_shapes=[pltpu.VMEM((groups, LANES, LANES), F32)],
        compiler_params=_params("parallel", "parallel", "arbitrary"),
        name=name,
    )(*args)
    return z.reshape(nb * seq, da), sout


def _conv_kernel(glu_ref, prev_ref, w_ref, cb_ref, lng_ref, lnb_ref, h_ref, st_ref, xbuf, dwbuf, *, taps, tt):
    t = pl.program_id(1)
    ncb = xbuf.shape[0]
    hal = taps - 1
    top = 32 - hal
    rb = min(tt, 64)

    for cb in range(ncb):
        cs = slice(cb * LANES, (cb + 1) * LANES)

        @pl.when(t == 0)
        def _():
            xbuf[cb, top:32, :] = prev_ref[0, :, cs]

        @pl.when(t > 0)
        def _():
            xbuf[cb, top:32, :] = xbuf[cb, tt + top:tt + 32, :]

        xbuf[cb, 32:32 + tt, :] = glu_ref[0, :, cs].astype(F32)

    def body(cb, carry):
        for r0 in range(0, tt, rb):
            acc = jnp.zeros((rb, LANES), F32) + cb_ref[cb]
            for kx in range(taps):
                acc = acc + xbuf[cb, r0 + top + kx:r0 + top + kx + rb, :] * w_ref[cb, kx:kx + 1, :]
            dwbuf[cb, r0:r0 + rb, :] = acc
        return carry

    lax.fori_loop(0, ncb, body, 0)

    s1 = jnp.zeros((tt, LANES), F32)
    for cb in range(ncb):
        s1 = s1 + dwbuf[cb]
    dch = ncb * LANES
    mean = jnp.sum(s1, axis=-1, keepdims=True) * (1.0 / dch)
    s2 = jnp.zeros((tt, LANES), F32)
    for cb in range(ncb):
        d = dwbuf[cb] - mean
        s2 = s2 + d * d
    var = jnp.sum(s2, axis=-1, keepdims=True) * (1.0 / dch)
    inv = lax.rsqrt(var + EPS_LN)
    for cb in range(ncb):
        cs = slice(cb * LANES, (cb + 1) * LANES)
        f = (dwbuf[cb] - mean) * inv * lng_ref[:, cs] + lnb_ref[:, cs]
        h_ref[0, :, cs] = (f * _sigmoid(f)).astype(h_ref.dtype)

    @pl.when(t == pl.num_programs(1) - 1)
    def _():
        for cb in range(ncb):
            st_ref[0, :, cb * LANES:(cb + 1) * LANES] = xbuf[cb, tt + top:tt + 32, :]


def _conv(glu, conv_prev, conv_w, conv_b, ln_g, ln_b, nb, seq, name):
    dc = glu.shape[1]
    taps = conv_w.shape[0]
    assert taps - 1 <= 32
    ncb = dc // LANES
    tt = _tile(seq, 256)
    assert tt >= taps - 1
    w3 = jnp.pad(conv_w, ((0, 32 - taps), (0, 0))).reshape(32, ncb, LANES).transpose(1, 0, 2)
    cb3 = conv_b.reshape(ncb, 1, LANES)
    kern = functools.partial(_conv_kernel, taps=taps, tt=tt)
    h, st = pl.pallas_call(
        kern,
        grid=(nb, seq // tt),
        in_specs=[pl.BlockSpec((1, tt, dc), lambda b, t: (b, t, 0)),
                  pl.BlockSpec((1, taps - 1, dc), lambda b, t: (b, 0, 0)),
                  pl.BlockSpec((ncb, 32, LANES), lambda b, t: (0, 0, 0)),
                  pl.BlockSpec((ncb, 1, LANES), lambda b, t: (0, 0, 0)),
                  pl.BlockSpec((1, dc), lambda b, t: (0, 0)),
                  pl.BlockSpec((1, dc), lambda b, t: (0, 0))],
        out_specs=[pl.BlockSpec((1, tt, dc), lambda b, t: (b, t, 0)),
                   pl.BlockSpec((1, taps - 1, dc), lambda b, t: (b, 0, 0))],
        out_shape=[jax.ShapeDtypeStruct((nb, seq, dc), BF16),
                   jax.ShapeDtypeStruct((nb, taps - 1, dc), F32)],
        scratch_shapes=[pltpu.VMEM((ncb, tt + 32, LANES), F32),
                        pltpu.VMEM((ncb, tt, LANES), F32)],
        compiler_params=_params("parallel", "arbitrary"),
        name=name,
    )(glu.reshape(nb, seq, dc), conv_prev, w3, cb3, ln_g, ln_b)
    return h.reshape(nb * seq, dc), st


def _prepare_layer(p, l):
    d = p["w_in"].shape[1]
    da = p["w0"].shape[1]
    rw, ra, rg = p["w_up"].shape[1], p["a_up"].shape[1], p["g_up"].shape[1]
    rwp, rap, rgp = _rup(rw, LANES), _rup(ra, LANES), _rup(rg, LANES)
    p_rwkv = 3 * da + rw + ra + rg
    dc = p["conv_w"].shape[2]
    w_in = p["w_in"][l]
    mu = p["mu_shift"][l]
    o = 3 * da
    padc = lambda a, n: jnp.pad(a, ((0, 0), (0, n - a.shape[1])))
    padr = lambda a, n: jnp.pad(a, ((0, n - a.shape[0]), (0, 0)))
    w_lora = jnp.concatenate([padc(w_in[:, o:o + rw], rwp), padc(w_in[:, o + rw:o + rw + ra], rap),
                              padc(w_in[:, o + rw + ra:p_rwkv], rgp)], axis=1)
    mu2 = mu.reshape(1, -1)
    mu_lora = jnp.concatenate([padc(mu2[:, o:o + rw], rwp), padc(mu2[:, o + rw:o + rw + ra], rap),
                               padc(mu2[:, o + rw + ra:p_rwkv], rgp)], axis=1)
    row = lambda a: a.reshape(1, -1).astype(F32)
    q = dict(
        rw=rw, ra=ra, rg=rg, rwp=rwp, rap=rap, rgp=rgp, da=da, dc=dc,
        ffn1_norm=p["ffn1_norm"][l], ffn2_norm=p["ffn2_norm"][l], mix_norm=p["mix_norm"][l],
        ffn1_w1=p["ffn1_w1"][l].astype(BF16), ffn1_w3=p["ffn1_w3"][l].astype(BF16),
        ffn1_w2=p["ffn1_w2"][l].astype(BF16),
        ffn2_w1=p["ffn2_w1"][l].astype(BF16), ffn2_w3=p["ffn2_w3"][l].astype(BF16),
        ffn2_w2=p["ffn2_w2"][l].astype(BF16),
        w_rkv=w_in[:, :o].astype(BF16), mu_rkv=mu2[:, :o],
        w_lora=w_lora.astype(BF16), mu_lora=mu_lora,
        w_cv=w_in[:, p_rwkv:p_rwkv + dc].astype(BF16),
        w_cg=w_in[:, p_rwkv + dc:p_rwkv + 2 * dc].astype(BF16),
        b_cv=row(p["b_conv_in"][l][:dc]), b_cg=row(p["b_conv_in"][l][dc:]),
        w_gate=w_in[:, p_rwkv + 2 * dc:].astype(BF16), b_gate=row(p["b_gate"][l]),
        w_up=padr(p["w_up"][l], rwp).astype(BF16), a_up=padr(p["a_up"][l], rap).astype(BF16),
        g_up=padr(p["g_up"][l], rgp).astype(BF16),
        w0=row(p["w0"][l]), a0=row(p["a0"][l]), k_k=row(p["k_k"][l]), k_a=row(p["k_a"][l]),
        r_k=row(p["r_k"][l]), gn_g=row(p["gn_g"][l]), gn_b=row(p["gn_b"][l]),
        w_o_a=p["w_o_a"][l].astype(BF16),
        conv_w=p["conv_w"][l], conv_b=p["conv_b"][l],
        conv_ln_g=row(p["conv_ln_g"][l]), conv_ln_b=row(p["conv_ln_b"][l]),
        w_o_c=p["w_o_c"][l].astype(BF16), b_o_c=row(p["b_o_c"][l]),
        w_out=p["w_out"][l].astype(BF16),
    )
    return q


def _pad_shift(shift, q):
    da, rw, ra = q["da"], q["rw"], q["ra"]
    o = 3 * da
    padc = lambda a, n: jnp.pad(a, ((0, 0), (0, n - a.shape[1])))
    lora = jnp.concatenate([padc(shift[:, o:o + rw], q["rwp"]), padc(shift[:, o + rw:o + rw + ra], q["rap"]),
                            padc(shift[:, o + rw + ra:], q["rgp"])], axis=1)
    return shift[:, None, :o], lora[:, None, :]


def _unpad_shift(last_rkv, last_lora, q):
    rw, ra, rg, rwp, rap = q["rw"], q["ra"], q["rg"], q["rwp"], q["rap"]
    ll = last_lora[:, 0]
    return jnp.concatenate([last_rkv[:, 0], ll[:, :rw], ll[:, rwp:rwp + ra],
                            ll[:, rwp + rap:rwp + rap + rg]], axis=1)


def _layer(x, wkv_prev, shift_prev, conv_prev, q, nb, seq, final_norm, tag):
    da, dc = q["da"], q["dc"]
    x, u = _ffn(x, q["ffn1_norm"], q["ffn1_w1"], q["ffn1_w3"], q["ffn1_w2"], q["mix_norm"], BF16,
                True, True, name=f"ffn1_{tag}")

    sp_rkv, sp_lora = _pad_shift(shift_prev, q)
    xm_rkv, last_rkv = _proj_shift(u, q["w_rkv"], q["mu_rkv"], sp_rkv, seq, BF16, name=f"proj_rkv_{tag}")
    xm_lora, last_lora = _proj_shift(u, q["w_lora"], q["mu_lora"], sp_lora, seq, F32,
                                     name=f"proj_lora_{tag}")
    shift_new = _unpad_shift(last_rkv, last_lora, q)

    glu = _mm(u, [q["w_cv"], q["w_cg"]],
              lambda accs, rows, tiles: (accs[0] + rows[0]) * _sigmoid(accs[1] + rows[1]),
              rows=[(q["b_cv"], 0), (q["b_cg"], 0)], out_dtype=BF16, tn_pref=512, name=f"proj_glu_{tag}")
    gates = _mm(u, [q["w_gate"]], lambda accs, rows, tiles: _sigmoid(accs[0] + rows[0]),
                rows=[(q["b_gate"], 0)], out_dtype=BF16, name=f"proj_gate_{tag}")

    z, wkv_new = _wkv(xm_rkv, xm_lora, q, wkv_prev, nb, seq, name=f"wkv_{tag}")
    y_a = _mm(z, [q["w_o_a"]], lambda accs, rows, tiles: accs[0], name=f"out_a_{tag}")

    hc, conv_new = _conv(glu, conv_prev, q["conv_w"], q["conv_b"], q["conv_ln_g"], q["conv_ln_b"],
                         nb, seq, name=f"conv_{tag}")
    mixed = _mm(hc, [q["w_o_c"]],
                lambda accs, rows, tiles: tiles[0] * tiles[2] + tiles[1] * (accs[0] + rows[0]),
                rows=[(q["b_o_c"], 0)], tiles=[(gates, 0), (gates, da), (y_a, 0)],
                out_dtype=BF16, name=f"out_c_mix_{tag}")
    x = _mm(mixed, [q["w_out"]], lambda accs, rows, tiles: tiles[0] + accs[0],
            tiles=[(x, 0)], name=f"out_{tag}")
    last = final_norm is not None
    x, y = _ffn(x, q["ffn2_norm"], q["ffn2_w1"], q["ffn2_w3"], q["ffn2_w2"],
                final_norm if last else q["ffn2_norm"], F32, not last, last, name=f"ffn2_{tag}")
    return (y if last else x), wkv_new, shift_new, conv_new


def _trunk(x, wkv0, shift0, conv0, layers, final_norm, tag):
    nb, seq, d = x.shape
    xf = x.reshape(nb * seq, d)
    wkvs, shifts, convs = [], [], []
    for l, q in enumerate(layers):
        fin = final_norm if l == len(layers) - 1 else None
        xf, s_wkv, s_shift, s_conv = _layer(xf, None if wkv0 is None else wkv0[l], shift0[l], conv0[l],
                                            q, nb, seq, fin, f"{tag}{l}")
        wkvs.append(s_wkv)
        shifts.append(s_shift)
        convs.append(s_conv)
    return xf.reshape(nb, seq, d), jnp.stack(wkvs), jnp.stack(shifts), jnp.stack(convs)


def kernel(x_prompt, x_sample, state_wkv, state_shift, state_conv, ffn1_norm, ffn1_w1, ffn1_w3, ffn1_w2, mix_norm, w_in, mu_shift, w0, w_up, a0, a_up, g_up, k_k, k_a, r_k, gn_g, gn_b, w_o_a, b_conv_in, conv_w, conv_b, conv_ln_g, conv_ln_b, w_o_c, b_o_c, b_gate, w_out, ffn2_norm, ffn2_w1, ffn2_w3, ffn2_w2, final_norm):
    p = dict(ffn1_norm=ffn1_norm, ffn1_w1=ffn1_w1, ffn1_w3=ffn1_w3, ffn1_w2=ffn1_w2,
             mix_norm=mix_norm, w_in=w_in, mu_shift=mu_shift, w0=w0, w_up=w_up, a0=a0,
             a_up=a_up, g_up=g_up, k_k=k_k, k_a=k_a, r_k=r_k, gn_g=gn_g, gn_b=gn_b,
             w_o_a=w_o_a, b_conv_in=b_conv_in, conv_w=conv_w, conv_b=conv_b,
             conv_ln_g=conv_ln_g, conv_ln_b=conv_ln_b, w_o_c=w_o_c, b_o_c=b_o_c,
             b_gate=b_gate, w_out=w_out, ffn2_norm=ffn2_norm, ffn2_w1=ffn2_w1,
             ffn2_w3=ffn2_w3, ffn2_w2=ffn2_w2)
    depth = w_in.shape[0]
    layers = [_prepare_layer(p, l) for l in range(depth)]
    bp = x_prompt.shape[0]
    p_rwkv = state_shift.shape[2]
    taps = conv_w.shape[1]
    dc = conv_w.shape[2]
    shift0 = jnp.zeros((depth, bp, p_rwkv), F32)
    conv0 = jnp.zeros((depth, bp, taps - 1, dc), F32)
    y_p, wkv_p, shift_p, conv_p = _trunk(x_prompt, None, shift0, conv0, layers, final_norm, "p")
    y_s, wkv_s, shift_s, conv_s = _trunk(x_sample, state_wkv, state_shift, state_conv, layers,
                                         final_norm, "s")
    return (y_p, y_s, wkv_p, shift_p, conv_p, wkv_s, shift_s, conv_s)
```

```python
import functools

import jax
import jax.numpy as jnp
from jax import lax
from jax.experimental import pallas as pl
from jax.experimental.pallas import tpu as pltpu

F32 = jnp.float32
BF16 = jnp.bfloat16

HEAD = 64
LANES = 128
CHUNK = 64
EPS_RMS = 1e-6
EPS_LN = 1e-5
EPS_GN = 64e-5
VMEM_LIMIT = 56 * 1024 * 1024


def _rup(a, b):
    return -(-a // b) * b


def _tile(n, pref):
    if n <= pref:
        return n
    t = pref
    while n % t:
        t //= 2
    return t


def _params(*sem):
    return pltpu.CompilerParams(dimension_semantics=sem, vmem_limit_bytes=VMEM_LIMIT)


def _dot(a, b):
    return jnp.dot(a.astype(BF16), b.astype(BF16), preferred_element_type=F32)


def _dot_nt(a, b):
    return lax.dot_general(a.astype(BF16), b.astype(BF16), (((1,), (1,)), ((), ())),
                           preferred_element_type=F32)


def _sigmoid(x):
    return 1.0 / (1.0 + jnp.exp(-x))


def _ffn_kernel(x_ref, g_ref, w1_ref, w3_ref, w2_ref, g2_ref, *refs, emit_x, emit_n):
    h_ref, acc_ref = refs[-2:]
    f = pl.program_id(1)

    @pl.when(f == 0)
    def _():
        x = x_ref[...]
        ms = jnp.mean(x * x, axis=-1, keepdims=True)
        h_ref[...] = (x * lax.rsqrt(ms + EPS_RMS) * g_ref[...]).astype(BF16)
        acc_ref[...] = jnp.zeros_like(acc_ref)

    h = h_ref[...]
    a = jnp.dot(h, w1_ref[...], preferred_element_type=F32)
    b = jnp.dot(h, w3_ref[...], preferred_element_type=F32)
    gated = (a * _sigmoid(a)) * b
    acc_ref[...] += jnp.dot(gated.astype(BF16), w2_ref[...], preferred_element_type=F32)

    @pl.when(f == pl.num_programs(1) - 1)
    def _():
        o = x_ref[...] + 0.5 * acc_ref[...]
        if emit_x:
            refs[0][...] = o
        if emit_n:
            n_ref = refs[1 if emit_x else 0]
            ms = jnp.mean(o * o, axis=-1, keepdims=True)
            n_ref[...] = (o * lax.rsqrt(ms + EPS_RMS) * g2_ref[...]).astype(n_ref.dtype)


def _ffn(x, g, w1, w3, w2, g2, n_dtype, emit_x, emit_n, name):
    m, d = x.shape
    dff = w1.shape[1]
    tm = _tile(m, 512)
    tf = _tile(dff, 512)
    tok = lambda: pl.BlockSpec((tm, d), lambda i, f: (i, 0))
    row = lambda: pl.BlockSpec((1, d), lambda i, f: (0, 0))
    out_specs = [tok()] * (int(emit_x) + int(emit_n))
    out_shape = (([jax.ShapeDtypeStruct((m, d), F32)] if emit_x else [])
                 + ([jax.ShapeDtypeStruct((m, d), n_dtype)] if emit_n else []))
    outs = pl.pallas_call(
        functools.partial(_ffn_kernel, emit_x=emit_x, emit_n=emit_n),
        grid=(m // tm, dff // tf),
        in_specs=[tok(), row(),
                  pl.BlockSpec((d, tf), lambda i, f: (0, f)),
                  pl.BlockSpec((d, tf), lambda i, f: (0, f)),
                  pl.BlockSpec((tf, d), lambda i, f: (f, 0)),
                  row()],
        out_specs=out_specs,
        out_shape=out_shape,
        scratch_shapes=[pltpu.VMEM((tm, d), BF16), pltpu.VMEM((tm, d), F32)],
        compiler_params=_params("parallel", "arbitrary"),
        name=name,
    )(x, g.reshape(1, d), w1, w3, w2, g2.reshape(1, d))
    outs = list(outs)
    return (outs.pop(0) if emit_x else None), (outs.pop(0) if emit_n else None)


def _mm_kernel(*refs, n_a, n_w, n_row, n_tile, epi):
    a_refs = refs[:n_a]
    w_refs = refs[n_a:n_a + n_w]
    row_refs = refs[n_a + n_w:n_a + n_w + n_row]
    tile_refs = refs[n_a + n_w + n_row:n_a + n_w + n_row + n_tile]
    o_ref = refs[-1]
    accs = [jnp.dot(a_refs[i % n_a][...], w[...], preferred_element_type=F32) for i, w in enumerate(w_refs)]
    out = epi(accs, [r[...] for r in row_refs], [t[...] for t in tile_refs])
    o_ref[...] = out.astype(o_ref.dtype)


def _mm(a, ws, epi, rows=(), tiles=(), out_dtype=F32, tm_pref=512, tn_pref=2048, name="mm"):
    a_list = list(a) if isinstance(a, (list, tuple)) else [a]
    assert len(a_list) in (1, len(ws))
    m, k = a_list[0].shape
    n = ws[0].shape[1]
    tm = _tile(m, tm_pref)
    tn = _tile(n, tn_pref)
    in_specs = [pl.BlockSpec((tm, k), lambda j, i: (i, 0)) for _ in a_list]
    in_specs += [pl.BlockSpec((k, tn), lambda j, i: (0, j)) for _ in ws]
    args = [*a_list, *ws]
    for arr, off in rows:
        in_specs.append(pl.BlockSpec((1, tn), lambda j, i, o=off // tn: (0, j + o)))
        args.append(arr)
    for arr, off in tiles:
        in_specs.append(pl.BlockSpec((tm, tn), lambda j, i, o=off // tn: (i, j + o)))
        args.append(arr)
    kern = functools.partial(_mm_kernel, n_a=len(a_list), n_w=len(ws), n_row=len(rows), n_tile=len(tiles),
                             epi=epi)
    return pl.pallas_call(
        kern,
        grid=(n // tn, m // tm),
        in_specs=in_specs,
        out_specs=pl.BlockSpec((tm, tn), lambda j, i: (i, j)),
        out_shape=jax.ShapeDtypeStruct((m, n), out_dtype),
        compiler_params=_params("parallel", "parallel"),
        name=name,
    )(*args)


def _proj_shift_kernel(u_ref, w_ref, mu_ref, sprev_ref, xm_ref, last_ref, carry_ref, *, seq, tm):
    i = pl.program_id(1)
    p = jnp.dot(u_ref[...], w_ref[...], preferred_element_type=F32)
    row = lax.broadcasted_iota(jnp.int32, p.shape, 0)
    prev = pltpu.roll(p, 1, 0)
    if tm >= seq:
        per_tile = tm // seq
        for s in range(per_tile):
            b = i * per_tile + s
            prev = jnp.where(row == s * seq, sprev_ref[b], prev)
            last_ref[b] = p[(s + 1) * seq - 1:(s + 1) * seq, :]
    else:
        tiles_per_seq = seq // tm
        b = i // tiles_per_seq
        t = i % tiles_per_seq
        first = jnp.where(t == 0, sprev_ref[b], carry_ref[...])
        prev = jnp.where(row == 0, first, prev)
        carry_ref[...] = p[tm - 1:tm, :]

        @pl.when(t == tiles_per_seq - 1)
        def _():
            last_ref[b] = p[tm - 1:tm, :]

    xm_ref[...] = (p + (prev - p) * mu_ref[...]).astype(xm_ref.dtype)


def _proj_shift(u, w, mu, sprev, seq, out_dtype, name):
    m, k = u.shape
    n = w.shape[1]
    nb = m // seq
    tn = _tile(n, 2048)
    tm = _tile(m, 512)
    if tm < seq:
        assert seq % tm == 0
    else:
        assert tm % seq == 0
    kern = functools.partial(_proj_shift_kernel, seq=seq, tm=tm)
    return pl.pallas_call(
        kern,
        grid=(n // tn, m // tm),
        in_specs=[pl.BlockSpec((tm, k), lambda j, i: (i, 0)),
                  pl.BlockSpec((k, tn), lambda j, i: (0, j)),
                  pl.BlockSpec((1, tn), lambda j, i: (0, j)),
                  pl.BlockSpec((nb, 1, tn), lambda j, i: (0, 0, j))],
        out_specs=[pl.BlockSpec((tm, tn), lambda j, i: (i, j)),
                   pl.BlockSpec((nb, 1, tn), lambda j, i: (0, 0, j))],
        out_shape=[jax.ShapeDtypeStruct((m, n), out_dtype),
                   jax.ShapeDtypeStruct((nb, 1, n), F32)],
        scratch_shapes=[pltpu.VMEM((1, tn), F32)],
        compiler_params=_params("arbitrary", "arbitrary"),
        name=name,
    )(u, w, mu, sprev)


def _pair_sums(x):
    h0 = lax.broadcasted_iota(jnp.int32, x.shape, 1) < HEAD
    s0 = jnp.sum(jnp.where(h0, x, 0.0), axis=-1, keepdims=True)
    s1 = jnp.sum(jnp.where(h0, 0.0, x), axis=-1, keepdims=True)
    return jnp.where(h0, s0, s1)


def _stack_heads(x):
    lane = lax.broadcasted_iota(jnp.int32, x.shape, 1)
    h0 = (lane & HEAD) == 0
    return jnp.concatenate([jnp.where(h0, x, 0.0), jnp.where(h0, 0.0, x)], axis=0)


def _wkv_kernel(*refs, groups, sub, rwp, rap, has_state):
    (xr_ref, xk_ref, xv_ref, xl_ref, wup_ref, aup_ref, gup_ref,
     w0_ref, a0_ref, kkw_ref, kaw_ref, rk_ref, gng_ref, gnb_ref) = refs[:14]
    s0_ref = refs[14] if has_state else None
    z_ref, sout_ref, st_ref = refs[-3:]
    c = pl.program_id(2)
    C, W = CHUNK, LANES
    G = range(groups)
    sls = [slice(g * W, (g + 1) * W) for g in G]

    @pl.when(c == 0)
    def _():
        if has_state:
            zero = jnp.zeros((HEAD, HEAD), F32)
            for g in G:
                top = jnp.concatenate([s0_ref[0, 2 * g], zero], axis=1)
                bot = jnp.concatenate([zero, s0_ref[0, 2 * g + 1]], axis=1)
                st_ref[g] = jnp.concatenate([top, bot], axis=0)
        else:
            st_ref[...] = jnp.zeros_like(st_ref)

    lane = lax.broadcasted_iota(jnp.int32, (C, W), 1)
    row = lax.broadcasted_iota(jnp.int32, (C, W), 0)
    s_idx = lane & (HEAD - 1)
    strict = s_idx < row
    incl = s_idx <= row
    eye = jnp.where(s_idx == row, 1.0, 0.0)
    same_head = ((lax.broadcasted_iota(jnp.int32, (W, W), 0) < HEAD)
                 == (lax.broadcasted_iota(jnp.int32, (W, W), 1) < HEAD))
    tri = jnp.where(lax.broadcasted_iota(jnp.int32, (C, C), 1)
                    <= lax.broadcasted_iota(jnp.int32, (C, C), 0), 1.0, 0.0).astype(BF16)

    def cumsum(x):
        p1 = x.astype(BF16)
        r1 = x - p1.astype(F32)
        p2 = r1.astype(BF16)
        p3 = (r1 - p2.astype(F32)).astype(BF16)
        return (jnp.dot(tri, p1, preferred_element_type=F32)
                + jnp.dot(tri, p2, preferred_element_type=F32)
                + jnp.dot(tri, p3, preferred_element_type=F32))

    class Chunk:
        N_STAGES = 10

        def __init__(self, ci):
            self.rows = slice(ci * C, (ci + 1) * C)
            self.p = [dict() for _ in G]

        def head_common(self):
            xl = xl_ref[0, self.rows, :]
            dw = _dot(jnp.tanh(xl[:, :rwp]), wup_ref[...])
            da = _dot(xl[:, rwp:rwp + rap], aup_ref[...])
            self.gate = _dot(_sigmoid(xl[:, rwp + rap:]), gup_ref[...])
            zz = -(w0_ref[...] + dw)
            softplus = jnp.maximum(zz, 0.0) + jnp.log(1.0 + jnp.exp(-jnp.abs(zz)))
            self.lw = -jnp.exp(-softplus - 0.5)
            self.alpha = _sigmoid(a0_ref[...] + da)
            self.cum = cumsum(self.lw)

        def head(self, g):
            sl, p = sls[g], self.p[g]
            r = xr_ref[0, self.rows, sl].astype(F32)
            v = xv_ref[0, self.rows, sl].astype(F32)
            k_raw = xk_ref[0, self.rows, sl].astype(F32)
            alpha = self.alpha[:, sl]
            k = k_raw * (1.0 + (alpha - 1.0) * kaw_ref[:, sl])
            kk = k_raw * kkw_ref[:, sl]
            kk = kk * lax.rsqrt(jnp.maximum(_pair_sums(kk * kk), 1e-24))
            bb = kk * alpha
            cum = self.cum[:, sl]
            cl = cum[C - 1:C, :]
            e_neg = jnp.exp(-cum)
            e_hat = jnp.exp(cl - cum)
            at = -(kk * jnp.exp(cum - self.lw[:, sl]))
            rt = r * jnp.exp(cum)
            p.update(
                v=v, at=at, rt=rt.astype(BF16), g_end=jnp.exp(cl), rkr=r * k * rk_ref[:, sl],
                lhs=jnp.concatenate([at, rt], axis=0).astype(BF16),
                rhs=jnp.concatenate([_stack_heads(bb * e_neg), _stack_heads(k * e_neg)], axis=0).astype(BF16),
                bk=jnp.concatenate([bb * e_hat, k * e_hat], axis=0).astype(BF16),
                sv=_stack_heads(v).astype(BF16))

        def stage(self, n, s):
            for g in G:
                p = self.p[g]
                if n == 0:
                    m1 = _dot_nt(p["lhs"], p["rhs"])
                    p["ll"] = jnp.where(strict, m1[:C, :W], 0.0)
                    p["aak"] = jnp.where(strict, m1[:C, W:], 0.0).astype(BF16)
                    p["arbk"] = jnp.concatenate([jnp.where(incl, m1[C:, :W], 0.0),
                                                 jnp.where(incl, m1[C:, W:], 0.0)], axis=1).astype(BF16)
                elif n == 1:
                    p["x"] = jnp.concatenate([p["at"], _dot(p["aak"], p["sv"])], axis=1)
                    p["tinv"] = eye + p["ll"]
                    p["lk"] = _dot(p["ll"], _stack_heads(p["ll"]))
                elif n <= 6:
                    p["tinv"] = p["tinv"] + _dot(p["lk"], _stack_heads(p["tinv"]))
                    if n < 6:
                        p["lk"] = _dot(p["lk"], _stack_heads(p["lk"]))
                elif n == 7:
                    p["x"] = _dot(p["tinv"], _stack_heads(p["x"]))
                elif n == 8:
                    uy = _dot_nt(jnp.concatenate([p["x"][:, :W].astype(BF16), p["rt"]], axis=0), s[g])
                    p["u"] = uy[:C] + p["x"][:, W:]
                    p["y"] = uy[C:]
                else:
                    u = p["u"]
                    ds = _dot(jnp.concatenate([u, p["v"]], axis=0).T, p["bk"])
                    s[g] = s[g] * p["g_end"] + jnp.where(same_head, ds, 0.0)
                    p["y"] = p["y"] + _dot(p["arbk"],
                                           jnp.concatenate([_stack_heads(u).astype(BF16), p["sv"]], axis=0))

        def post(self, g):
            sl, p = sls[g], self.p[g]
            y = p["y"]
            d = y - _pair_sums(y) * (1.0 / HEAD)
            var = _pair_sums(d * d) * (1.0 / HEAD)
            yn = d * lax.rsqrt(var + EPS_GN) * gng_ref[:, sl] + gnb_ref[:, sl]
            bonus = _pair_sums(p["rkr"]) * p["v"]
            z_ref[0, self.rows, sl] = ((yn + bonus) * self.gate[:, sl]).astype(z_ref.dtype)

    chunks = [Chunk(ci) for ci in range(sub)]
    s = [st_ref[g] for g in G]
    chunks[0].head_common()
    for g in G:
        chunks[0].head(g)
    for ci, ch in enumerate(chunks):
        side = []
        if ci > 0:
            side += [functools.partial(chunks[ci - 1].post, g) for g in G]
        if ci + 1 < sub:
            side += [chunks[ci + 1].head_common] + [functools.partial(chunks[ci + 1].head, g) for g in G]
        for n in range(Chunk.N_STAGES):
            ch.stage(n, s)
            for piece in side[len(side) * n // Chunk.N_STAGES:len(side) * (n + 1) // Chunk.N_STAGES]:
                piece()
    for g in G:
        chunks[-1].post(g)
        st_ref[g] = s[g]

    @pl.when(c == pl.num_programs(2) - 1)
    def _():
        for g in G:
            sg = st_ref[g]
            sout_ref[0, 2 * g] = sg[:HEAD, :HEAD]
            sout_ref[0, 2 * g + 1] = pltpu.roll(sg[HEAD:, :], HEAD, 1)[:, :HEAD]


def _wkv(xm_rkv, xm_lora, q, wkv_prev, nb, seq, name):
    da = q["da"]
    lora = xm_lora.shape[1]
    pairs = da // LANES
    groups = next(n for n in (16, 8, 4, 2, 1) if pairs % n == 0)
    sub = next(n for n in (4, 2, 1) if seq % (n * CHUNK) == 0)
    gw = groups * LANES
    ncol = da // gw
    tt = sub * CHUNK
    assert seq % tt == 0
    has_state = wkv_prev is not None
    r3 = xm_rkv.reshape(nb, seq, 3 * da)
    tok = lambda off: pl.BlockSpec((1, tt, gw), lambda b, j, c, o=off: (b, c, j + o))
    col = lambda rows: pl.BlockSpec((rows, gw), lambda b, j, c: (0, j))
    st = lambda: pl.BlockSpec((1, 2 * groups, HEAD, HEAD), lambda b, j, c: (b, j, 0, 0))
    in_specs = [tok(0), tok(ncol), tok(2 * ncol),
                pl.BlockSpec((1, tt, lora), lambda b, j, c: (b, c, 0)),
                col(q["rwp"]), col(q["rap"]), col(q["rgp"])] + [col(1)] * 7
    args = [r3, r3, r3, xm_lora.reshape(nb, seq, lora), q["w_up"], q["a_up"], q["g_up"],
            q["w0"], q["a0"], q["k_k"], q["k_a"], q["r_k"], q["gn_g"], q["gn_b"]]
    if has_state:
        in_specs.append(st())
        args.append(wkv_prev)
    kern = functools.partial(_wkv_kernel, groups=groups, sub=sub, rwp=q["rwp"], rap=q["rap"],
                             has_state=has_state)
    z, sout = pl.pallas_call(
        kern,
        grid=(nb, ncol, seq // tt),
        in_specs=in_specs,
        out_specs=[pl.BlockSpec((1, tt, gw), lambda b, j, c: (b, c, j)), st()],
        out_shape=[jax.ShapeDtypeStruct((nb, seq, da), BF16),
                   jax.ShapeDtypeStruct((nb, 2 * pairs, HEAD, HEAD), F32)],
        scratch_shapes=[pltpu.VMEM((groups, LANES, LANES), F32)],
        compiler_params=_params("parallel", "parallel", "arbitrary"),
        name=name,
    )(*args)
    return z.reshape(nb * seq, da), sout


def _conv_kernel(glu_ref, prev_ref, w_ref, cb_ref, lng_ref, lnb_ref, h_ref, st_ref, xbuf, dwbuf, *, taps, tt):
    t = pl.program_id(1)
    ncb = xbuf.shape[0]
    hal = taps - 1
    top = 32 - hal
    rb = min(tt, 64)

    for cb in range(ncb):
        cs = slice(cb * LANES, (cb + 1) * LANES)

        @pl.when(t == 0)
        def _():
            xbuf[cb, top:32, :] = prev_ref[0, :, cs]

        @pl.when(t > 0)
        def _():
            xbuf[cb, top:32, :] = xbuf[cb, tt + top:tt + 32, :]

        xbuf[cb, 32:32 + tt, :] = glu_ref[0, :, cs].astype(F32)

    def body(cb, carry):
        for r0 in range(0, tt, rb):
            acc = jnp.zeros((rb, LANES), F32) + cb_ref[cb]
            for kx in range(taps):
                acc = acc + xbuf[cb, r0 + top + kx:r0 + top + kx + rb, :] * w_ref[cb, kx:kx + 1, :]
            dwbuf[cb, r0:r0 + rb, :] = acc
        return carry

    lax.fori_loop(0, ncb, body, 0)

    s1 = jnp.zeros((tt, LANES), F32)
    for cb in range(ncb):
        s1 = s1 + dwbuf[cb]
    dch = ncb * LANES
    mean = jnp.sum(s1, axis=-1, keepdims=True) * (1.0 / dch)
    s2 = jnp.zeros((tt, LANES), F32)
    for cb in range(ncb):
        d = dwbuf[cb] - mean
        s2 = s2 + d * d
    var = jnp.sum(s2, axis=-1, keepdims=True) * (1.0 / dch)
    inv = lax.rsqrt(var + EPS_LN)
    for cb in range(ncb):
        cs = slice(cb * LANES, (cb + 1) * LANES)
        f = (dwbuf[cb] - mean) * inv * lng_ref[:, cs] + lnb_ref[:, cs]
        h_ref[0, :, cs] = (f * _sigmoid(f)).astype(h_ref.dtype)

    @pl.when(t == pl.num_programs(1) - 1)
    def _():
        for cb in range(ncb):
            st_ref[0, :, cb * LANES:(cb + 1) * LANES] = xbuf[cb, tt + top:tt + 32, :]


def _conv(glu, conv_prev, conv_w, conv_b, ln_g, ln_b, nb, seq, name):
    dc = glu.shape[1]
    taps = conv_w.shape[0]
    assert taps - 1 <= 32
    ncb = dc // LANES
    tt = _tile(seq, 256)
    assert tt >= taps - 1
    w3 = jnp.pad(conv_w, ((0, 32 - taps), (0, 0))).reshape(32, ncb, LANES).transpose(1, 0, 2)
    cb3 = conv_b.reshape(ncb, 1, LANES)
    kern = functools.partial(_conv_kernel, taps=taps, tt=tt)
    h, st = pl.pallas_call(
        kern,
        grid=(nb, seq // tt),
        in_specs=[pl.BlockSpec((1, tt, dc), lambda b, t: (b, t, 0)),
                  pl.BlockSpec((1, taps - 1, dc), lambda b, t: (b, 0, 0)),
                  pl.BlockSpec((ncb, 32, LANES), lambda b, t: (0, 0, 0)),
                  pl.BlockSpec((ncb, 1, LANES), lambda b, t: (0, 0, 0)),
                  pl.BlockSpec((1, dc), lambda b, t: (0, 0)),
                  pl.BlockSpec((1, dc), lambda b, t: (0, 0))],
        out_specs=[pl.BlockSpec((1, tt, dc), lambda b, t: (b, t, 0)),
                   pl.BlockSpec((1, taps - 1, dc), lambda b, t: (b, 0, 0))],
        out_shape=[jax.ShapeDtypeStruct((nb, seq, dc), BF16),
                   jax.ShapeDtypeStruct((nb, taps - 1, dc), F32)],
        scratch_shapes=[pltpu.VMEM((ncb, tt + 32, LANES), F32),
                        pltpu.VMEM((ncb, tt, LANES), F32)],
        compiler_params=_params("parallel", "arbitrary"),
        name=name,
    )(glu.reshape(nb, seq, dc), conv_prev, w3, cb3, ln_g, ln_b)
    return h.reshape(nb * seq, dc), st


def _prepare_layer(p, l):
    d = p["w_in"].shape[1]
    da = p["w0"].shape[1]
    rw, ra, rg = p["w_up"].shape[1], p["a_up"].shape[1], p["g_up"].shape[1]
    rwp, rap, rgp = _rup(rw, LANES), _rup(ra, LANES), _rup(rg, LANES)
    p_rwkv = 3 * da + rw + ra + rg
    dc = p["conv_w"].shape[2]
    w_in = p["w_in"][l]
    mu = p["mu_shift"][l]
    o = 3 * da
    padc = lambda a, n: jnp.pad(a, ((0, 0), (0, n - a.shape[1])))
    padr = lambda a, n: jnp.pad(a, ((0, n - a.shape[0]), (0, 0)))
    w_lora = jnp.concatenate([padc(w_in[:, o:o + rw], rwp), padc(w_in[:, o + rw:o + rw + ra], rap),
                              padc(w_in[:, o + rw + ra:p_rwkv], rgp)], axis=1)
    mu2 = mu.reshape(1, -1)
    mu_lora = jnp.concatenate([padc(mu2[:, o:o + rw], rwp), padc(mu2[:, o + rw:o + rw + ra], rap),
                               padc(mu2[:, o + rw + ra:p_rwkv], rgp)], axis=1)
    row = lambda a: a.reshape(1, -1).astype(F32)
    q = dict(
        rw=rw, ra=ra, rg=rg, rwp=rwp, rap=rap, rgp=rgp, da=da, dc=dc,
        ffn1_norm=p["ffn1_norm"][l], ffn2_norm=p["ffn2_norm"][l], mix_norm=p["mix_norm"][l],
        ffn1_w1=p["ffn1_w1"][l].astype(BF16), ffn1_w3=p["ffn1_w3"][l].astype(BF16),
        ffn1_w2=p["ffn1_w2"][l].astype(BF16),
        ffn2_w1=p["ffn2_w1"][l].astype(BF16), ffn2_w3=p["ffn2_w3"][l].astype(BF16),
        ffn2_w2=p["ffn2_w2"][l].astype(BF16),
        w_rkv=w_in[:, :o].astype(BF16), mu_rkv=mu2[:, :o],
        w_lora=w_lora.astype(BF16), mu_lora=mu_lora,
        w_cv=w_in[:, p_rwkv:p_rwkv + dc].astype(BF16),
        w_cg=w_in[:, p_rwkv + dc:p_rwkv + 2 * dc].astype(BF16),
        b_cv=row(p["b_conv_in"][l][:dc]), b_cg=row(p["b_conv_in"][l][dc:]),
        w_gate=w_in[:, p_rwkv + 2 * dc:].astype(BF16), b_gate=row(p["b_gate"][l]),
        w_up=padr(p["w_up"][l], rwp).astype(BF16), a_up=padr(p["a_up"][l], rap).astype(BF16),
        g_up=padr(p["g_up"][l], rgp).astype(BF16),
        w0=row(p["w0"][l]), a0=row(p["a0"][l]), k_k=row(p["k_k"][l]), k_a=row(p["k_a"][l]),
        r_k=row(p["r_k"][l]), gn_g=row(p["gn_g"][l]), gn_b=row(p["gn_b"][l]),
        w_o_a=p["w_o_a"][l].astype(BF16),
        conv_w=p["conv_w"][l], conv_b=p["conv_b"][l],
        conv_ln_g=row(p["conv_ln_g"][l]), conv_ln_b=row(p["conv_ln_b"][l]),
        w_o_c=p["w_o_c"][l].astype(BF16), b_o_c=row(p["b_o_c"][l]),
        w_out=p["w_out"][l].astype(BF16),
    )
    return q


def _pad_shift(shift, q):
    da, rw, ra = q["da"], q["rw"], q["ra"]
    o = 3 * da
    padc = lambda a, n: jnp.pad(a, ((0, 0), (0, n - a.shape[1])))
    lora = jnp.concatenate([padc(shift[:, o:o + rw], q["rwp"]), padc(shift[:, o + rw:o + rw + ra], q["rap"]),
                            padc(shift[:, o + rw + ra:], q["rgp"])], axis=1)
    return shift[:, None, :o], lora[:, None, :]


def _unpad_shift(last_rkv, last_lora, q):
    rw, ra, rg, rwp, rap = q["rw"], q["ra"], q["rg"], q["rwp"], q["rap"]
    ll = last_lora[:, 0]
    return jnp.concatenate([last_rkv[:, 0], ll[:, :rw], ll[:, rwp:rwp + ra],
                            ll[:, rwp + rap:rwp + rap + rg]], axis=1)


def _layer(x, wkv_prev, shift_prev, conv_prev, q, nb, seq, final_norm, tag):
    da, dc = q["da"], q["dc"]
    x, u = _ffn(x, q["ffn1_norm"], q["ffn1_w1"], q["ffn1_w3"], q["ffn1_w2"], q["mix_norm"], BF16,
                True, True, name=f"ffn1_{tag}")

    sp_rkv, sp_lora = _pad_shift(shift_prev, q)
    xm_rkv, last_rkv = _proj_shift(u, q["w_rkv"], q["mu_rkv"], sp_rkv, seq, BF16, name=f"proj_rkv_{tag}")
    xm_lora, last_lora = _proj_shift(u, q["w_lora"], q["mu_lora"], sp_lora, seq, F32,
                                     name=f"proj_lora_{tag}")
    shift_new = _unpad_shift(last_rkv, last_lora, q)

    glu = _mm(u, [q["w_cv"], q["w_cg"]],
              lambda accs, rows, tiles: (accs[0] + rows[0]) * _sigmoid(accs[1] + rows[1]),
              rows=[(q["b_cv"], 0), (q["b_cg"], 0)], out_dtype=BF16, tn_pref=1024, name=f"proj_glu_{tag}")
    gates = _mm(u, [q["w_gate"]], lambda accs, rows, tiles: _sigmoid(accs[0] + rows[0]),
                rows=[(q["b_gate"], 0)], out_dtype=BF16, name=f"proj_gate_{tag}")

    z, wkv_new = _wkv(xm_rkv, xm_lora, q, wkv_prev, nb, seq, name=f"wkv_{tag}")

    hc, conv_new = _conv(glu, conv_prev, q["conv_w"], q["conv_b"], q["conv_ln_g"], q["conv_ln_b"],
                         nb, seq, name=f"conv_{tag}")
    mixed = _mm([z, hc], [q["w_o_a"], q["w_o_c"]],
                lambda accs, rows, tiles: tiles[0] * accs[0] + tiles[1] * (accs[1] + rows[0]),
                rows=[(q["b_o_c"], 0)], tiles=[(gates, 0), (gates, da)],
                out_dtype=BF16, tn_pref=1024, name=f"out_mix_{tag}")
    x = _mm(mixed, [q["w_out"]], lambda accs, rows, tiles: tiles[0] + accs[0],
            tiles=[(x, 0)], name=f"out_{tag}")
    last = final_norm is not None
    x, y = _ffn(x, q["ffn2_norm"], q["ffn2_w1"], q["ffn2_w3"], q["ffn2_w2"],
                final_norm if last else q["ffn2_norm"], F32, not last, last, name=f"ffn2_{tag}")
    return (y if last else x), wkv_new, shift_new, conv_new


def _trunk(x, wkv0, shift0, conv0, layers, final_norm, tag):
    nb, seq, d = x.shape
    xf = x.reshape(nb * seq, d)
    wkvs, shifts, convs = [], [], []
    for l, q in enumerate(layers):
        fin = final_norm if l == len(layers) - 1 else None
        xf, s_wkv, s_shift, s_conv = _layer(xf, None if wkv0 is None else wkv0[l], shift0[l], conv0[l],
                                            q, nb, seq, fin, f"{tag}{l}")
        wkvs.append(s_wkv)
        shifts.append(s_shift)
        convs.append(s_conv)
    return xf.reshape(nb, seq, d), jnp.stack(wkvs), jnp.stack(shifts), jnp.stack(convs)


def kernel(x_prompt, x_sample, state_wkv, state_shift, state_conv, ffn1_norm, ffn1_w1, ffn1_w3, ffn1_w2, mix_norm, w_in, mu_shift, w0, w_up, a0, a_up, g_up, k_k, k_a, r_k, gn_g, gn_b, w_o_a, b_conv_in, conv_w, conv_b, conv_ln_g, conv_ln_b, w_o_c, b_o_c, b_gate, w_out, ffn2_norm, ffn2_w1, ffn2_w3, ffn2_w2, final_norm):
    p = dict(ffn1_norm=ffn1_norm, ffn1_w1=ffn1_w1, ffn1_w3=ffn1_w3, ffn1_w2=ffn1_w2,
             mix_norm=mix_norm, w_in=w_in, mu_shift=mu_shift, w0=w0, w_up=w_up, a0=a0,
             a_up=a_up, g_up=g_up, k_k=k_k, k_a=k_a, r_k=r_k, gn_g=gn_g, gn_b=gn_b,
             w_o_a=w_o_a, b_conv_in=b_conv_in, conv_w=conv_w, conv_b=conv_b,
             conv_ln_g=conv_ln_g, conv_ln_b=conv_ln_b, w_o_c=w_o_c, b_o_c=b_o_c,
             b_gate=b_gate, w_out=w_out, ffn2_norm=ffn2_norm, ffn2_w1=ffn2_w1,
             ffn2_w3=ffn2_w3, ffn2_w2=ffn2_w2)
    depth = w_in.shape[0]
    layers = [_prepare_layer(p, l) for l in range(depth)]
    bp = x_prompt.shape[0]
    p_rwkv = state_shift.shape[2]
    taps = conv_w.shape[1]
    dc = conv_w.shape[2]
    shift0 = jnp.zeros((depth, bp, p_rwkv), F32)
    conv0 = jnp.zeros((depth, bp, taps - 1, dc), F32)
    y_p, wkv_p, shift_p, conv_p = _trunk(x_prompt, None, shift0, conv0, layers, final_norm, "p")
    y_s, wkv_s, shift_s, conv_s = _trunk(x_sample, state_wkv, state_shift, state_conv, layers,
                                         final_norm, "s")
    return (y_p, y_s, wkv_p, shift_p, conv_p, wkv_s, shift_s, conv_s)
```

```python
import functools

import jax
import jax.numpy as jnp
from jax import lax
from jax.experimental import pallas as pl
from jax.experimental.pallas import tpu as pltpu

F32 = jnp.float32
BF16 = jnp.bfloat16

HEAD = 64
LANES = 128
CHUNK = 64
EPS_RMS = 1e-6
EPS_LN = 1e-5
EPS_GN = 64e-5
VMEM_LIMIT = 56 * 1024 * 1024


def _rup(a, b):
    return -(-a // b) * b


def _tile(n, pref):
    if n <= pref:
        return n
    t = pref
    while n % t:
        t //= 2
    return t


def _params(*sem):
    return pltpu.CompilerParams(dimension_semantics=sem, vmem_limit_bytes=VMEM_LIMIT)


def _dot(a, b):
    return jnp.dot(a.astype(BF16), b.astype(BF16), preferred_element_type=F32)


def _dot_nt(a, b):
    return lax.dot_general(a.astype(BF16), b.astype(BF16), (((1,), (1,)), ((), ())),
                           preferred_element_type=F32)


def _sigmoid(x):
    return 1.0 / (1.0 + jnp.exp(-x))


def _ffn_kernel(x_ref, g_ref, w1_ref, w3_ref, w2_ref, g2_ref, *refs, emit_x, emit_n, n_hidden):
    h_ref, acc_ref = refs[-2:]
    f = pl.program_id(1)

    def part(h):
        a = jnp.dot(h, w1_ref[...], preferred_element_type=F32)
        b = jnp.dot(h, w3_ref[...], preferred_element_type=F32)
        gated = (a * _sigmoid(a)) * b
        return jnp.dot(gated.astype(BF16), w2_ref[...], preferred_element_type=F32)

    def normed():
        x = x_ref[...]
        ms = jnp.mean(x * x, axis=-1, keepdims=True)
        return (x * lax.rsqrt(ms + EPS_RMS) * g_ref[...]).astype(BF16)

    def finish(acc):
        o = x_ref[...] + 0.5 * acc
        if emit_x:
            refs[0][...] = o
        if emit_n:
            n_ref = refs[1 if emit_x else 0]
            ms = jnp.mean(o * o, axis=-1, keepdims=True)
            n_ref[...] = (o * lax.rsqrt(ms + EPS_RMS) * g2_ref[...]).astype(n_ref.dtype)

    if n_hidden == 1:
        finish(part(normed()))
        return

    @pl.when(f == 0)
    def _():
        h = normed()
        h_ref[...] = h
        acc_ref[...] = part(h)

    @pl.when(jnp.logical_and(f > 0, f < n_hidden - 1))
    def _():
        acc_ref[...] += part(h_ref[...])

    @pl.when(f == n_hidden - 1)
    def _():
        finish(acc_ref[...] + part(h_ref[...]))


def _ffn(x, g, w1, w3, w2, g2, n_dtype, emit_x, emit_n, name):
    m, d = x.shape
    dff = w1.shape[1]
    tm = _tile(m, 512)
    tf = _tile(dff, 512)
    tok = lambda: pl.BlockSpec((tm, d), lambda i, f: (i, 0))
    row = lambda: pl.BlockSpec((1, d), lambda i, f: (0, 0))
    out_specs = [tok()] * (int(emit_x) + int(emit_n))
    out_shape = (([jax.ShapeDtypeStruct((m, d), F32)] if emit_x else [])
                 + ([jax.ShapeDtypeStruct((m, d), n_dtype)] if emit_n else []))
    outs = pl.pallas_call(
        functools.partial(_ffn_kernel, emit_x=emit_x, emit_n=emit_n, n_hidden=dff // tf),
        grid=(m // tm, dff // tf),
        in_specs=[tok(), row(),
                  pl.BlockSpec((d, tf), lambda i, f: (0, f)),
                  pl.BlockSpec((d, tf), lambda i, f: (0, f)),
                  pl.BlockSpec((tf, d), lambda i, f: (f, 0)),
                  row()],
        out_specs=out_specs,
        out_shape=out_shape,
        scratch_shapes=[pltpu.VMEM((tm, d), BF16), pltpu.VMEM((tm, d), F32)],
        compiler_params=_params("parallel", "arbitrary"),
        name=name,
    )(x, g.reshape(1, d), w1, w3, w2, g2.reshape(1, d))
    outs = list(outs)
    return (outs.pop(0) if emit_x else None), (outs.pop(0) if emit_n else None)


def _mm_kernel(*refs, n_a, n_w, n_row, n_tile, epi):
    a_refs = refs[:n_a]
    w_refs = refs[n_a:n_a + n_w]
    row_refs = refs[n_a + n_w:n_a + n_w + n_row]
    tile_refs = refs[n_a + n_w + n_row:n_a + n_w + n_row + n_tile]
    o_ref = refs[-1]
    accs = [jnp.dot(a_refs[i % n_a][...], w[...], preferred_element_type=F32) for i, w in enumerate(w_refs)]
    out = epi(accs, [r[...] for r in row_refs], [t[...] for t in tile_refs])
    o_ref[...] = out.astype(o_ref.dtype)


def _mm(a, ws, epi, rows=(), tiles=(), out_dtype=F32, tm_pref=512, tn_pref=2048, name="mm"):
    a_list = list(a) if isinstance(a, (list, tuple)) else [a]
    assert len(a_list) in (1, len(ws))
    m, k = a_list[0].shape
    n = ws[0].shape[1]
    tm = _tile(m, tm_pref)
    tn = _tile(n, tn_pref)
    in_specs = [pl.BlockSpec((tm, k), lambda j, i: (i, 0)) for _ in a_list]
    in_specs += [pl.BlockSpec((k, tn), lambda j, i: (0, j)) for _ in ws]
    args = [*a_list, *ws]
    for arr, off in rows:
        in_specs.append(pl.BlockSpec((1, tn), lambda j, i, o=off // tn: (0, j + o)))
        args.append(arr)
    for arr, off in tiles:
        in_specs.append(pl.BlockSpec((tm, tn), lambda j, i, o=off // tn: (i, j + o)))
        args.append(arr)
    kern = functools.partial(_mm_kernel, n_a=len(a_list), n_w=len(ws), n_row=len(rows), n_tile=len(tiles),
                             epi=epi)
    return pl.pallas_call(
        kern,
        grid=(n // tn, m // tm),
        in_specs=in_specs,
        out_specs=pl.BlockSpec((tm, tn), lambda j, i: (i, j)),
        out_shape=jax.ShapeDtypeStruct((m, n), out_dtype),
        compiler_params=_params("parallel", "parallel"),
        name=name,
    )(*args)


def _proj_shift_kernel(u_ref, w_ref, mu_ref, sprev_ref, xm_ref, last_ref, carry_ref, *, seq, tm):
    i = pl.program_id(1)
    p = jnp.dot(u_ref[...], w_ref[...], preferred_element_type=F32)
    row = lax.broadcasted_iota(jnp.int32, p.shape, 0)
    prev = pltpu.roll(p, 1, 0)
    if tm >= seq:
        per_tile = tm // seq
        for s in range(per_tile):
            b = i * per_tile + s
            prev = jnp.where(row == s * seq, sprev_ref[b], prev)
            last_ref[b] = p[(s + 1) * seq - 1:(s + 1) * seq, :]
    else:
        tiles_per_seq = seq // tm
        b = i // tiles_per_seq
        t = i % tiles_per_seq
        first = jnp.where(t == 0, sprev_ref[b], carry_ref[...])
        prev = jnp.where(row == 0, first, prev)
        carry_ref[...] = p[tm - 1:tm, :]

        @pl.when(t == tiles_per_seq - 1)
        def _():
            last_ref[b] = p[tm - 1:tm, :]

    xm_ref[...] = (p + (prev - p) * mu_ref[...]).astype(xm_ref.dtype)


def _proj_shift(u, w, mu, sprev, seq, out_dtype, name):
    m, k = u.shape
    n = w.shape[1]
    nb = m // seq
    tn = _tile(n, 2048)
    tm = _tile(m, 512)
    if tm < seq:
        assert seq % tm == 0
    else:
        assert tm % seq == 0
    kern = functools.partial(_proj_shift_kernel, seq=seq, tm=tm)
    return pl.pallas_call(
        kern,
        grid=(n // tn, m // tm),
        in_specs=[pl.BlockSpec((tm, k), lambda j, i: (i, 0)),
                  pl.BlockSpec((k, tn), lambda j, i: (0, j)),
                  pl.BlockSpec((1, tn), lambda j, i: (0, j)),
                  pl.BlockSpec((nb, 1, tn), lambda j, i: (0, 0, j))],
        out_specs=[pl.BlockSpec((tm, tn), lambda j, i: (i, j)),
                   pl.BlockSpec((nb, 1, tn), lambda j, i: (0, 0, j))],
        out_shape=[jax.ShapeDtypeStruct((m, n), out_dtype),
                   jax.ShapeDtypeStruct((nb, 1, n), F32)],
        scratch_shapes=[pltpu.VMEM((1, tn), F32)],
        compiler_params=_params("arbitrary", "arbitrary"),
        name=name,
    )(u, w, mu, sprev)


def _pair_sums(x):
    h0 = lax.broadcasted_iota(jnp.int32, x.shape, 1) < HEAD
    s0 = jnp.sum(jnp.where(h0, x, 0.0), axis=-1, keepdims=True)
    s1 = jnp.sum(jnp.where(h0, 0.0, x), axis=-1, keepdims=True)
    return jnp.where(h0, s0, s1)


def _stack_heads(x):
    lane = lax.broadcasted_iota(jnp.int32, x.shape, 1)
    h0 = (lane & HEAD) == 0
    return jnp.concatenate([jnp.where(h0, x, 0.0), jnp.where(h0, 0.0, x)], axis=0)


def _wkv_kernel(*refs, groups, sub, rwp, rap, has_state):
    (xr_ref, xk_ref, xv_ref, xl_ref, wup_ref, aup_ref, gup_ref,
     w0_ref, a0_ref, kkw_ref, kaw_ref, rk_ref, gng_ref, gnb_ref) = refs[:14]
    s0_ref = refs[14] if has_state else None
    z_ref, sout_ref, st_ref = refs[-3:]
    c = pl.program_id(2)
    C, W = CHUNK, LANES
    G = range(groups)
    sls = [slice(g * W, (g + 1) * W) for g in G]

    @pl.when(c == 0)
    def _():
        if has_state:
            zero = jnp.zeros((HEAD, HEAD), F32)
            for g in G:
                top = jnp.concatenate([s0_ref[0, 2 * g], zero], axis=1)
                bot = jnp.concatenate([zero, s0_ref[0, 2 * g + 1]], axis=1)
                st_ref[g] = jnp.concatenate([top, bot], axis=0)
        else:
            st_ref[...] = jnp.zeros_like(st_ref)

    lane = lax.broadcasted_iota(jnp.int32, (C, W), 1)
    row = lax.broadcasted_iota(jnp.int32, (C, W), 0)
    s_idx = lane & (HEAD - 1)
    strict = s_idx < row
    incl = s_idx <= row
    eye = jnp.where(s_idx == row, 1.0, 0.0)
    same_head = ((lax.broadcasted_iota(jnp.int32, (W, W), 0) < HEAD)
                 == (lax.broadcasted_iota(jnp.int32, (W, W), 1) < HEAD))
    tri = jnp.where(lax.broadcasted_iota(jnp.int32, (C, C), 1)
                    <= lax.broadcasted_iota(jnp.int32, (C, C), 0), 1.0, 0.0).astype(BF16)

    def cumsum(x):
        p1 = x.astype(BF16)
        r1 = x - p1.astype(F32)
        p2 = r1.astype(BF16)
        p3 = (r1 - p2.astype(F32)).astype(BF16)
        return (jnp.dot(tri, p1, preferred_element_type=F32)
                + jnp.dot(tri, p2, preferred_element_type=F32)
                + jnp.dot(tri, p3, preferred_element_type=F32))

    class Chunk:
        N_STAGES = 10

        def __init__(self, ci):
            self.rows = slice(ci * C, (ci + 1) * C)
            self.p = [dict() for _ in G]

        def head_common(self):
            xl = xl_ref[0, self.rows, :]
            dw = _dot(jnp.tanh(xl[:, :rwp]), wup_ref[...])
            da = _dot(xl[:, rwp:rwp + rap], aup_ref[...])
            self.gate = _dot(_sigmoid(xl[:, rwp + rap:]), gup_ref[...])
            zz = -(w0_ref[...] + dw)
            softplus = jnp.maximum(zz, 0.0) + jnp.log(1.0 + jnp.exp(-jnp.abs(zz)))
            self.lw = -jnp.exp(-softplus - 0.5)
            self.alpha = _sigmoid(a0_ref[...] + da)
            self.cum = cumsum(self.lw)

        def head(self, g):
            sl, p = sls[g], self.p[g]
            r = xr_ref[0, self.rows, sl].astype(F32)
            v = xv_ref[0, self.rows, sl].astype(F32)
            k_raw = xk_ref[0, self.rows, sl].astype(F32)
            alpha = self.alpha[:, sl]
            k = k_raw * (1.0 + (alpha - 1.0) * kaw_ref[:, sl])
            kk = k_raw * kkw_ref[:, sl]
            kk = kk * lax.rsqrt(jnp.maximum(_pair_sums(kk * kk), 1e-24))
            bb = kk * alpha
            cum = self.cum[:, sl]
            cl = cum[C - 1:C, :]
            e_neg = jnp.exp(-cum)
            e_hat = jnp.exp(cl - cum)
            at = -(kk * jnp.exp(cum - self.lw[:, sl]))
            rt = r * jnp.exp(cum)
            p.update(
                v=v, at=at, rt=rt.astype(BF16), g_end=jnp.exp(cl), rkr=r * k * rk_ref[:, sl],
                lhs=jnp.concatenate([at, rt], axis=0).astype(BF16),
                rhs=jnp.concatenate([_stack_heads(bb * e_neg), _stack_heads(k * e_neg)], axis=0).astype(BF16),
                bk=jnp.concatenate([bb * e_hat, k * e_hat], axis=0).astype(BF16),
                sv=_stack_heads(v).astype(BF16))

        def stage(self, n, s):
            for g in G:
                p = self.p[g]
                if n == 0:
                    m1 = _dot_nt(p["lhs"], p["rhs"])
                    p["ll"] = jnp.where(strict, m1[:C, :W], 0.0)
                    p["aak"] = jnp.where(strict, m1[:C, W:], 0.0).astype(BF16)
                    p["arbk"] = jnp.concatenate([jnp.where(incl, m1[C:, :W], 0.0),
                                                 jnp.where(incl, m1[C:, W:], 0.0)], axis=1).astype(BF16)
                elif n == 1:
                    p["x"] = jnp.concatenate([p["at"], _dot(p["aak"], p["sv"])], axis=1)
                    p["tinv"] = eye + p["ll"]
                    p["lk"] = _dot(p["ll"], _stack_heads(p["ll"]))
                elif n <= 6:
                    p["tinv"] = p["tinv"] + _dot(p["lk"], _stack_heads(p["tinv"]))
                    if n < 6:
                        p["lk"] = _dot(p["lk"], _stack_heads(p["lk"]))
                elif n == 7:
                    p["x"] = _dot(p["tinv"], _stack_heads(p["x"]))
                elif n == 8:
                    uy = _dot_nt(jnp.concatenate([p["x"][:, :W].astype(BF16), p["rt"]], axis=0), s[g])
                    p["u"] = uy[:C] + p["x"][:, W:]
                    p["y"] = uy[C:]
                else:
                    u = p["u"]
                    ds = _dot(jnp.concatenate([u, p["v"]], axis=0).T, p["bk"])
                    s[g] = s[g] * p["g_end"] + jnp.where(same_head, ds, 0.0)
                    p["y"] = p["y"] + _dot(p["arbk"],
                                           jnp.concatenate([_stack_heads(u).astype(BF16), p["sv"]], axis=0))

        def post(self, g):
            sl, p = sls[g], self.p[g]
            y = p["y"]
            d = y - _pair_sums(y) * (1.0 / HEAD)
            var = _pair_sums(d * d) * (1.0 / HEAD)
            yn = d * lax.rsqrt(var + EPS_GN) * gng_ref[:, sl] + gnb_ref[:, sl]
            bonus = _pair_sums(p["rkr"]) * p["v"]
            z_ref[0, self.rows, sl] = ((yn + bonus) * self.gate[:, sl]).astype(z_ref.dtype)

    chunks = [Chunk(ci) for ci in range(sub)]
    s = [st_ref[g] for g in G]
    chunks[0].head_common()
    for g in G:
        chunks[0].head(g)
    for ci, ch in enumerate(chunks):
        side = []
        if ci > 0:
            side += [functools.partial(chunks[ci - 1].post, g) for g in G]
        if ci + 1 < sub:
            side += [chunks[ci + 1].head_common] + [functools.partial(chunks[ci + 1].head, g) for g in G]
        for n in range(Chunk.N_STAGES):
            ch.stage(n, s)
            for piece in side[len(side) * n // Chunk.N_STAGES:len(side) * (n + 1) // Chunk.N_STAGES]:
                piece()
    for g in G:
        chunks[-1].post(g)
        st_ref[g] = s[g]

    @pl.when(c == pl.num_programs(2) - 1)
    def _():
        for g in G:
            sg = st_ref[g]
            sout_ref[0, 2 * g] = sg[:HEAD, :HEAD]
            sout_ref[0, 2 * g + 1] = pltpu.roll(sg[HEAD:, :], HEAD, 1)[:, :HEAD]


def _wkv(xm_rkv, xm_lora, q, wkv_prev, nb, seq, name):
    da = q["da"]
    lora = xm_lora.shape[1]
    pairs = da // LANES
    groups = next(n for n in (16, 8, 4, 2, 1) if pairs % n == 0)
    sub = next(n for n in (4, 2, 1) if seq % (n * CHUNK) == 0)
    gw = groups * LANES
    ncol = da // gw
    tt = sub * CHUNK
    assert seq % tt == 0
    has_state = wkv_prev is not None
    r3 = xm_rkv.reshape(nb, seq, 3 * da)
    tok = lambda off: pl.BlockSpec((1, tt, gw), lambda b, j, c, o=off: (b, c, j + o))
    col = lambda rows: pl.BlockSpec((rows, gw), lambda b, j, c: (0, j))
    st = lambda: pl.BlockSpec((1, 2 * groups, HEAD, HEAD), lambda b, j, c: (b, j, 0, 0))
    in_specs = [tok(0), tok(ncol), tok(2 * ncol),
                pl.BlockSpec((1, tt, lora), lambda b, j, c: (b, c, 0)),
                col(q["rwp"]), col(q["rap"]), col(q["rgp"])] + [col(1)] * 7
    args = [r3, r3, r3, xm_lora.reshape(nb, seq, lora), q["w_up"], q["a_up"], q["g_up"],
            q["w0"], q["a0"], q["k_k"], q["k_a"], q["r_k"], q["gn_g"], q["gn_b"]]
    if has_state:
        in_specs.append(st())
        args.append(wkv_prev)
    kern = functools.partial(_wkv_kernel, groups=groups, sub=sub, rwp=q["rwp"], rap=q["rap"],
                             has_state=has_state)
    z, sout = pl.pallas_call(
        kern,
        grid=(nb, ncol, seq // tt),
        in_specs=in_specs,
        out_specs=[pl.BlockSpec((1, tt, gw), lambda b, j, c: (b, c, j)), st()],
        out_shape=[jax.ShapeDtypeStruct((nb, seq, da), BF16),
                   jax.ShapeDtypeStruct((nb, 2 * pairs, HEAD, HEAD), F32)],
        scratch_shapes=[pltpu.VMEM((groups, LANES, LANES), F32)],
        compiler_params=_params("parallel", "parallel", "arbitrary"),
        name=name,
    )(*args)
    return z.reshape(nb * seq, da), sout


def _conv_kernel(glu_ref, prev_ref, w_ref, cb_ref, lng_ref, lnb_ref, h_ref, st_ref, xbuf, dwbuf, *, taps, tt):
    t = pl.program_id(1)
    ncb = xbuf.shape[0]
    hal = taps - 1
    top = 32 - hal
    rb = min(tt, 64)

    for cb in range(ncb):
        cs = slice(cb * LANES, (cb + 1) * LANES)

        @pl.when(t == 0)
        def _():
            xbuf[cb, top:32, :] = prev_ref[0, :, cs]

        @pl.when(t > 0)
        def _():
            xbuf[cb, top:32, :] = xbuf[cb, tt + top:tt + 32, :]

        xbuf[cb, 32:32 + tt, :] = glu_ref[0, :, cs].astype(F32)

    def body(cb, carry):
        for r0 in range(0, tt, rb):
            acc = jnp.zeros((rb, LANES), F32) + cb_ref[cb]
            for kx in range(taps):
                acc = acc + xbuf[cb, r0 + top + kx:r0 + top + kx + rb, :] * w_ref[cb, kx:kx + 1, :]
            dwbuf[cb, r0:r0 + rb, :] = acc
        return carry

    lax.fori_loop(0, ncb, body, 0)

    s1 = jnp.zeros((tt, LANES), F32)
    for cb in range(ncb):
        s1 = s1 + dwbuf[cb]
    dch = ncb * LANES
    mean = jnp.sum(s1, axis=-1, keepdims=True) * (1.0 / dch)
    s2 = jnp.zeros((tt, LANES), F32)
    for cb in range(ncb):
        d = dwbuf[cb] - mean
        s2 = s2 + d * d
    var = jnp.sum(s2, axis=-1, keepdims=True) * (1.0 / dch)
    inv = lax.rsqrt(var + EPS_LN)
    for cb in range(ncb):
        cs = slice(cb * LANES, (cb + 1) * LANES)
        f = (dwbuf[cb] - mean) * inv * lng_ref[:, cs] + lnb_ref[:, cs]
        h_ref[0, :, cs] = (f * _sigmoid(f)).astype(h_ref.dtype)

    @pl.when(t == pl.num_programs(1) - 1)
    def _():
        for cb in range(ncb):
            st_ref[0, :, cb * LANES:(cb + 1) * LANES] = xbuf[cb, tt + top:tt + 32, :]


def _conv(glu, conv_prev, conv_w, conv_b, ln_g, ln_b, nb, seq, name):
    dc = glu.shape[1]
    taps = conv_w.shape[0]
    assert taps - 1 <= 32
    ncb = dc // LANES
    tt = _tile(seq, 256)
    assert tt >= taps - 1
    w3 = jnp.pad(conv_w, ((0, 32 - taps), (0, 0))).reshape(32, ncb, LANES).transpose(1, 0, 2)
    cb3 = conv_b.reshape(ncb, 1, LANES)
    kern = functools.partial(_conv_kernel, taps=taps, tt=tt)
    h, st = pl.pallas_call(
        kern,
        grid=(nb, seq // tt),
        in_specs=[pl.BlockSpec((1, tt, dc), lambda b, t: (b, t, 0)),
                  pl.BlockSpec((1, taps - 1, dc), lambda b, t: (b, 0, 0)),
                  pl.BlockSpec((ncb, 32, LANES), lambda b, t: (0, 0, 0)),
                  pl.BlockSpec((ncb, 1, LANES), lambda b, t: (0, 0, 0)),
                  pl.BlockSpec((1, dc), lambda b, t: (0, 0)),
                  pl.BlockSpec((1, dc), lambda b, t: (0, 0))],
        out_specs=[pl.BlockSpec((1, tt, dc), lambda b, t: (b, t, 0)),
                   pl.BlockSpec((1, taps - 1, dc), lambda b, t: (b, 0, 0))],
        out_shape=[jax.ShapeDtypeStruct((nb, seq, dc), BF16),
                   jax.ShapeDtypeStruct((nb, taps - 1, dc), F32)],
        scratch_shapes=[pltpu.VMEM((ncb, tt + 32, LANES), F32),
                        pltpu.VMEM((ncb, tt, LANES), F32)],
        compiler_params=_params("parallel", "arbitrary"),
        name=name,
    )(glu.reshape(nb, seq, dc), conv_prev, w3, cb3, ln_g, ln_b)
    return h.reshape(nb * seq, dc), st


def _prepare_layer(p, l):
    d = p["w_in"].shape[1]
    da = p["w0"].shape[1]
    rw, ra, rg = p["w_up"].shape[1], p["a_up"].shape[1], p["g_up"].shape[1]
    rwp, rap, rgp = _rup(rw, LANES), _rup(ra, LANES), _rup(rg, LANES)
    p_rwkv = 3 * da + rw + ra + rg
    dc = p["conv_w"].shape[2]
    w_in = p["w_in"][l]
    mu = p["mu_shift"][l]
    o = 3 * da
    padc = lambda a, n: jnp.pad(a, ((0, 0), (0, n - a.shape[1])))
    padr = lambda a, n: jnp.pad(a, ((0, n - a.shape[0]), (0, 0)))
    w_lora = jnp.concatenate([padc(w_in[:, o:o + rw], rwp), padc(w_in[:, o + rw:o + rw + ra], rap),
                              padc(w_in[:, o + rw + ra:p_rwkv], rgp)], axis=1)
    mu2 = mu.reshape(1, -1)
    mu_lora = jnp.concatenate([padc(mu2[:, o:o + rw], rwp), padc(mu2[:, o + rw:o + rw + ra], rap),
                               padc(mu2[:, o + rw + ra:p_rwkv], rgp)], axis=1)
    row = lambda a: a.reshape(1, -1).astype(F32)
    q = dict(
        rw=rw, ra=ra, rg=rg, rwp=rwp, rap=rap, rgp=rgp, da=da, dc=dc,
        ffn1_norm=p["ffn1_norm"][l], ffn2_norm=p["ffn2_norm"][l], mix_norm=p["mix_norm"][l],
        ffn1_w1=p["ffn1_w1"][l].astype(BF16), ffn1_w3=p["ffn1_w3"][l].astype(BF16),
        ffn1_w2=p["ffn1_w2"][l].astype(BF16),
        ffn2_w1=p["ffn2_w1"][l].astype(BF16), ffn2_w3=p["ffn2_w3"][l].astype(BF16),
        ffn2_w2=p["ffn2_w2"][l].astype(BF16),
        w_rkv=w_in[:, :o].astype(BF16), mu_rkv=mu2[:, :o],
        w_lora=w_lora.astype(BF16), mu_lora=mu_lora,
        w_cv=w_in[:, p_rwkv:p_rwkv + dc].astype(BF16),
        w_cg=w_in[:, p_rwkv + dc:p_rwkv + 2 * dc].astype(BF16),
        b_cv=row(p["b_conv_in"][l][:dc]), b_cg=row(p["b_conv_in"][l][dc:]),
        w_gate=w_in[:, p_rwkv + 2 * dc:].astype(BF16), b_gate=row(p["b_gate"][l]),
        w_up=padr(p["w_up"][l], rwp).astype(BF16), a_up=padr(p["a_up"][l], rap).astype(BF16),
        g_up=padr(p["g_up"][l], rgp).astype(BF16),
        w0=row(p["w0"][l]), a0=row(p["a0"][l]), k_k=row(p["k_k"][l]), k_a=row(p["k_a"][l]),
        r_k=row(p["r_k"][l]), gn_g=row(p["gn_g"][l]), gn_b=row(p["gn_b"][l]),
        w_o_a=p["w_o_a"][l].astype(BF16),
        conv_w=p["conv_w"][l], conv_b=p["conv_b"][l],
        conv_ln_g=row(p["conv_ln_g"][l]), conv_ln_b=row(p["conv_ln_b"][l]),
        w_o_c=p["w_o_c"][l].astype(BF16), b_o_c=row(p["b_o_c"][l]),
        w_out=p["w_out"][l].astype(BF16),
    )
    return q


def _pad_shift(shift, q):
    da, rw, ra = q["da"], q["rw"], q["ra"]
    o = 3 * da
    padc = lambda a, n: jnp.pad(a, ((0, 0), (0, n - a.shape[1])))
    lora = jnp.concatenate([padc(shift[:, o:o + rw], q["rwp"]), padc(shift[:, o + rw:o + rw + ra], q["rap"]),
                            padc(shift[:, o + rw + ra:], q["rgp"])], axis=1)
    return shift[:, None, :o], lora[:, None, :]


def _unpad_shift(last_rkv, last_lora, q):
    rw, ra, rg, rwp, rap = q["rw"], q["ra"], q["rg"], q["rwp"], q["rap"]
    ll = last_lora[:, 0]
    return jnp.concatenate([last_rkv[:, 0], ll[:, :rw], ll[:, rwp:rwp + ra],
                            ll[:, rwp + rap:rwp + rap + rg]], axis=1)


def _layer(x, wkv_prev, shift_prev, conv_prev, q, nb, seq, final_norm, tag):
    da, dc = q["da"], q["dc"]
    x, u = _ffn(x, q["ffn1_norm"], q["ffn1_w1"], q["ffn1_w3"], q["ffn1_w2"], q["mix_norm"], BF16,
                True, True, name=f"ffn1_{tag}")

    sp_rkv, sp_lora = _pad_shift(shift_prev, q)
    xm_rkv, last_rkv = _proj_shift(u, q["w_rkv"], q["mu_rkv"], sp_rkv, seq, BF16, name=f"proj_rkv_{tag}")
    xm_lora, last_lora = _proj_shift(u, q["w_lora"], q["mu_lora"], sp_lora, seq, F32,
                                     name=f"proj_lora_{tag}")
    shift_new = _unpad_shift(last_rkv, last_lora, q)

    glu = _mm(u, [q["w_cv"], q["w_cg"]],
              lambda accs, rows, tiles: (accs[0] + rows[0]) * _sigmoid(accs[1] + rows[1]),
              rows=[(q["b_cv"], 0), (q["b_cg"], 0)], out_dtype=BF16, tn_pref=1024, name=f"proj_glu_{tag}")
    gates = _mm(u, [q["w_gate"]], lambda accs, rows, tiles: _sigmoid(accs[0] + rows[0]),
                rows=[(q["b_gate"], 0)], out_dtype=BF16, name=f"proj_gate_{tag}")

    z, wkv_new = _wkv(xm_rkv, xm_lora, q, wkv_prev, nb, seq, name=f"wkv_{tag}")

    hc, conv_new = _conv(glu, conv_prev, q["conv_w"], q["conv_b"], q["conv_ln_g"], q["conv_ln_b"],
                         nb, seq, name=f"conv_{tag}")
    mixed = _mm([z, hc], [q["w_o_a"], q["w_o_c"]],
                lambda accs, rows, tiles: tiles[0] * accs[0] + tiles[1] * (accs[1] + rows[0]),
                rows=[(q["b_o_c"], 0)], tiles=[(gates, 0), (gates, da)],
                out_dtype=BF16, tn_pref=1024, name=f"out_mix_{tag}")
    x = _mm(mixed, [q["w_out"]], lambda accs, rows, tiles: tiles[0] + accs[0],
            tiles=[(x, 0)], name=f"out_{tag}")
    last = final_norm is not None
    x, y = _ffn(x, q["ffn2_norm"], q["ffn2_w1"], q["ffn2_w3"], q["ffn2_w2"],
                final_norm if last else q["ffn2_norm"], F32, not last, last, name=f"ffn2_{tag}")
    return (y if last else x), wkv_new, shift_new, conv_new


def _trunk(x, wkv0, shift0, conv0, layers, final_norm, tag):
    nb, seq, d = x.shape
    xf = x.reshape(nb * seq, d)
    wkvs, shifts, convs = [], [], []
    for l, q in enumerate(layers):
        fin = final_norm if l == len(layers) - 1 else None
        xf, s_wkv, s_shift, s_conv = _layer(xf, None if wkv0 is None else wkv0[l], shift0[l], conv0[l],
                                            q, nb, seq, fin, f"{tag}{l}")
        wkvs.append(s_wkv)
        shifts.append(s_shift)
        convs.append(s_conv)
    return xf.reshape(nb, seq, d), jnp.stack(wkvs), jnp.stack(shifts), jnp.stack(convs)


def kernel(x_prompt, x_sample, state_wkv, state_shift, state_conv, ffn1_norm, ffn1_w1, ffn1_w3, ffn1_w2, mix_norm, w_in, mu_shift, w0, w_up, a0, a_up, g_up, k_k, k_a, r_k, gn_g, gn_b, w_o_a, b_conv_in, conv_w, conv_b, conv_ln_g, conv_ln_b, w_o_c, b_o_c, b_gate, w_out, ffn2_norm, ffn2_w1, ffn2_w3, ffn2_w2, final_norm):
    p = dict(ffn1_norm=ffn1_norm, ffn1_w1=ffn1_w1, ffn1_w3=ffn1_w3, ffn1_w2=ffn1_w2,
             mix_norm=mix_norm, w_in=w_in, mu_shift=mu_shift, w0=w0, w_up=w_up, a0=a0,
             a_up=a_up, g_up=g_up, k_k=k_k, k_a=k_a, r_k=r_k, gn_g=gn_g, gn_b=gn_b,
             w_o_a=w_o_a, b_conv_in=b_conv_in, conv_w=conv_w, conv_b=conv_b,
             conv_ln_g=conv_ln_g, conv_ln_b=conv_ln_b, w_o_c=w_o_c, b_o_c=b_o_c,
             b_gate=b_gate, w_out=w_out, ffn2_norm=ffn2_norm, ffn2_w1=ffn2_w1,
             ffn2_w3=ffn2_w3, ffn2_w2=ffn2_w2)
    depth = w_in.shape[0]
    layers = [_prepare_layer(p, l) for l in range(depth)]
    bp = x_prompt.shape[0]
    p_rwkv = state_shift.shape[2]
    taps = conv_w.shape[1]
    dc = conv_w.shape[2]
    shift0 = jnp.zeros((depth, bp, p_rwkv), F32)
    conv0 = jnp.zeros((depth, bp, taps - 1, dc), F32)
    y_p, wkv_p, shift_p, conv_p = _trunk(x_prompt, None, shift0, conv0, layers, final_norm, "p")
    y_s, wkv_s, shift_s, conv_s = _trunk(x_sample, state_wkv, state_shift, state_conv, layers,
                                         final_norm, "s")
    return (y_p, y_s, wkv_p, shift_p, conv_p, wkv_s, shift_s, conv_s)
```
